```python
import jax, jax.numpy as jnp
from jax import lax
import numpy as np

D_MODEL = 1024
BATCH = 4
SEQ = 4096
DEPTH = 1
DEC_BATCH = 32
DEC_SEQ = 4
PAST_LEN = 16384
PAGE_SIZE = 128

N_HEADS = 16
HEAD_DIM = 64
ATT_WIDTH = N_HEADS * HEAD_DIM
POOL_WINDOWS = (2, 4, 8, 16)
N_POOL_GROUPS = len(POOL_WINDOWS)
POOL_WIDTH = D_MODEL
POOL_GROUP = POOL_WIDTH // N_POOL_GROUPS
POOL_STATE = max(POOL_WINDOWS) - 1
D_FF = 4 * D_MODEL
PLE_DIM = 256
Q_BLOCK = 128
RMS_EPS = 1e-6
SB_BIAS_INIT = -7.0
IN_WIDTH = 3 * ATT_WIDTH + POOL_WIDTH + 2 * D_MODEL

kernel_name = 'stickbreak_pool_hybrid_decode_step'


def rms_norm(x, g):
    xf = x.astype(jnp.float32)
    y = xf * lax.rsqrt(jnp.mean(xf * xf, axis=-1, keepdims=True) + RMS_EPS)
    return (y * g.astype(jnp.float32)).astype(x.dtype)


def _project(h, g_norm, w_in):
    b, t = h.shape[0], h.shape[1]
    z = rms_norm(h, g_norm) @ w_in
    o = 0
    q = z[..., o:o + ATT_WIDTH].reshape(b, t, N_HEADS, HEAD_DIM); o += ATT_WIDTH
    k = z[..., o:o + ATT_WIDTH].reshape(b, t, N_HEADS, HEAD_DIM); o += ATT_WIDTH
    v = z[..., o:o + ATT_WIDTH].reshape(b, t, N_HEADS, HEAD_DIM); o += ATT_WIDTH
    u = z[..., o:o + POOL_WIDTH]; o += POOL_WIDTH
    ga = z[..., o:o + D_MODEL]; o += D_MODEL
    gb = z[..., o:o + D_MODEL]
    return q, k, v, u, ga, gb


def _sb_weights(z, mask, suffix):
    log_keep = jnp.where(mask, -jax.nn.softplus(z), 0.0)
    after = lax.cumsum(log_keep, axis=3, reverse=True) - log_keep + suffix[..., None]
    w = jnp.where(mask, jnp.exp(jax.nn.log_sigmoid(z) + after), 0.0)
    return w, jnp.sum(log_keep, axis=3)


def sb_attention_prompt(q, k, v, sb_bias):
    b, s = q.shape[0], q.shape[1]
    nblk = s // Q_BLOCK
    scale = HEAD_DIM ** -0.5
    kf = k.astype(jnp.float32)
    vf = v.astype(jnp.float32)
    bias = sb_bias.astype(jnp.float32)[None, :, None, None]
    kpos = jnp.arange(s)
    qb = q.reshape(b, nblk, Q_BLOCK, N_HEADS, HEAD_DIM).swapaxes(0, 1)

    def block(args):
        qi, i = args
        qpos = i * Q_BLOCK + jnp.arange(Q_BLOCK)
        z = jnp.einsum('bqhd,bkhd->bhqk', qi.astype(jnp.float32) * scale, kf) + bias
        mask = kpos[None, :] < qpos[:, None]
        w, _ = _sb_weights(z, mask, jnp.zeros(z.shape[:3], jnp.float32))
        return jnp.einsum('bhqk,bkhd->bqhd', w, vf)

    o = lax.map(block, (qb, jnp.arange(nblk)))
    return o.swapaxes(0, 1).reshape(b, s, ATT_WIDTH).astype(q.dtype)


def sb_attention_sample(q, k_new, v_new, k_past, v_past, sb_bias):
    b, t = q.shape[0], q.shape[1]
    qf = q.astype(jnp.float32) * (HEAD_DIM ** -0.5)
    bias = sb_bias.astype(jnp.float32)[None, :, None, None]
    pos = jnp.arange(t)
    z_new = jnp.einsum('bqhd,bkhd->bhqk', qf, k_new.astype(jnp.float32)) + bias
    w_new, tail = _sb_weights(z_new, pos[None, :] < pos[:, None], jnp.zeros(z_new.shape[:3], jnp.float32))
    z_past = jnp.einsum('bqhd,bkhd->bhqk', qf, k_past.astype(jnp.float32)) + bias
    w_past, _ = _sb_weights(z_past, True, tail)
    o = (jnp.einsum('bhqk,bkhd->bqhd', w_new, v_new.astype(jnp.float32))
         + jnp.einsum('bhqk,bkhd->bqhd', w_past, v_past.astype(jnp.float32)))
    return o.reshape(b, t, ATT_WIDTH).astype(q.dtype)


def pool_mix(u_ext, pos, w_pool, pool_scale):
    b, t = u_ext.shape[0], pos.shape[0]
    uf = u_ext.astype(jnp.float32)
    csum = jnp.concatenate([jnp.zeros_like(uf[:, :1]), jnp.cumsum(uf, axis=1)], axis=1)
    end = csum[:, POOL_STATE + 1:]
    u_new = uf[:, POOL_STATE:]
    outs = []
    for g, w in enumerate(POOL_WINDOWS):
        sl = slice(g * POOL_GROUP, (g + 1) * POOL_GROUP)
        start = csum[:, POOL_STATE + 1 - w:POOL_STATE + 1 - w + t, sl]
        cnt = jnp.minimum(pos + 1, w).astype(jnp.float32)[None, :, None]
        outs.append((end[..., sl] - start) / cnt - u_new[..., sl])
    d = jnp.stack(outs, axis=2)
    y = jnp.einsum('btgc,gce->btge', d, w_pool.astype(jnp.float32)).reshape(b, t, POOL_WIDTH)
    return (y * pool_scale.astype(jnp.float32)).astype(u_ext.dtype)


def _finish(h, o_att, o_pool, ga, gb, p, g_post_mix, g_pre_mlp, g_post_mlp,
            w_out, w_up, w_down, w_ple_proj, w_ple_gate):
    merged = jax.nn.sigmoid(ga) * o_att + jax.nn.sigmoid(gb) * o_pool
    h = h + rms_norm(merged @ w_out, g_post_mix)
    hid = jnp.square(jax.nn.relu(rms_norm(h, g_pre_mlp) @ w_up))
    h = h + rms_norm(hid @ w_down, g_post_mlp)
    return h + jax.nn.sigmoid(h @ w_ple_gate) * (p @ w_ple_proj)


def setup_inputs(seed: int = 0) -> dict:
    key = jax.random.key(seed)
    ks = jax.random.split(key, 24)
    f32 = jnp.float32
    n_pages = PAST_LEN // PAGE_SIZE
    n_used = DEC_BATCH * n_pages
    n_pool = n_used + n_used // 4
    nrm = lambda k, shape, s=1.0: (jax.random.normal(k, shape, f32) * s)
    gain = lambda k: 1.0 + 0.05 * jax.random.normal(k, (DEPTH, D_MODEL), f32)
    page_table = jax.random.permutation(ks[5], n_pool)[:n_used].reshape(DEC_BATCH, n_pages).astype(jnp.int32)
    return {
        'x_prompt': nrm(ks[0], (BATCH, SEQ, D_MODEL)),
        'x_sample': nrm(ks[1], (DEC_BATCH, DEC_SEQ, D_MODEL)),
        'cache_k': nrm(ks[2], (DEPTH, n_pool, PAGE_SIZE, N_HEADS, HEAD_DIM)),
        'cache_v': nrm(ks[3], (DEPTH, n_pool, PAGE_SIZE, N_HEADS, HEAD_DIM)),
        'state_pool': nrm(ks[4], (DEPTH, DEC_BATCH, POOL_STATE, POOL_WIDTH)),
        'page_table': page_table,
        'p_prompt': nrm(ks[6], (DEPTH, BATCH, SEQ, PLE_DIM)),
        'p_sample': nrm(ks[7], (DEPTH, DEC_BATCH, DEC_SEQ, PLE_DIM)),
        'norm_pre_mix': gain(ks[8]),
        'norm_post_mix': gain(ks[9]),
        'norm_pre_mlp': gain(ks[10]),
        'norm_post_mlp': gain(ks[11]),
        'w_in': nrm(ks[12], (DEPTH, D_MODEL, IN_WIDTH), D_MODEL ** -0.5),
        'sb_bias': SB_BIAS_INIT + 0.5 * jax.random.normal(ks[20], (DEPTH, N_HEADS), f32),
        'w_pool': nrm(ks[13], (DEPTH, N_POOL_GROUPS, POOL_GROUP, POOL_GROUP), POOL_GROUP ** -0.5),
        'pool_scale': 1.0 + 0.1 * jax.random.normal(ks[14], (DEPTH, POOL_WIDTH), f32),
        'w_out': nrm(ks[15], (DEPTH, D_MODEL, D_MODEL), D_MODEL ** -0.5),
        'w_up': nrm(ks[16], (DEPTH, D_MODEL, D_FF), D_MODEL ** -0.5),
        'w_down': nrm(ks[17], (DEPTH, D_FF, D_MODEL), D_FF ** -0.5),
        'w_ple_proj': nrm(ks[18], (DEPTH, PLE_DIM, D_MODEL), PLE_DIM ** -0.5),
        'w_ple_gate': nrm(ks[19], (DEPTH, D_MODEL, D_MODEL), D_MODEL ** -0.5),
    }


def reference(x_prompt, x_sample, cache_k, cache_v, state_pool, page_table, p_prompt, p_sample,
              norm_pre_mix, norm_post_mix, norm_pre_mlp, norm_post_mlp, w_in, sb_bias, w_pool, pool_scale,
              w_out, w_up, w_down, w_ple_proj, w_ple_gate):
    b, s = x_prompt.shape[0], x_prompt.shape[1]
    db, dt = x_sample.shape[0], x_sample.shape[1]
    past = page_table.shape[1] * PAGE_SIZE
    pos_prompt = jnp.arange(s)
    pos_sample = past + jnp.arange(dt)
    hp, hs = x_prompt, x_sample
    kp_l, vp_l, ks_l, vs_l, pp_l, ps_l = [], [], [], [], [], []
    for l in range(DEPTH):
        tail = (norm_post_mix[l], norm_pre_mlp[l], norm_post_mlp[l], w_out[l], w_up[l], w_down[l],
                w_ple_proj[l], w_ple_gate[l])
        q, k, v, u, ga, gb = _project(hp, norm_pre_mix[l], w_in[l])
        o_att = sb_attention_prompt(q, k, v, sb_bias[l])
        u_ext = jnp.concatenate([jnp.zeros((b, POOL_STATE, POOL_WIDTH), u.dtype), u], axis=1)
        o_pool = pool_mix(u_ext, pos_prompt, w_pool[l], pool_scale[l])
        hp = _finish(hp, o_att, o_pool, ga, gb, p_prompt[l], *tail)
        kp_l.append(k); vp_l.append(v); pp_l.append(u_ext[:, -POOL_STATE:])
        q, k, v, u, ga, gb = _project(hs, norm_pre_mix[l], w_in[l])
        k_past = cache_k[l][page_table].reshape(db, past, N_HEADS, HEAD_DIM)
        v_past = cache_v[l][page_table].reshape(db, past, N_HEADS, HEAD_DIM)
        o_att = sb_attention_sample(q, k, v, k_past, v_past, sb_bias[l])
        u_ext = jnp.concatenate([state_pool[l].astype(u.dtype), u], axis=1)
        o_pool = pool_mix(u_ext, pos_sample, w_pool[l], pool_scale[l])
        hs = _finish(hs, o_att, o_pool, ga, gb, p_sample[l], *tail)
        ks_l.append(k); vs_l.append(v); ps_l.append(u_ext[:, -POOL_STATE:])
    return (hp, hs, jnp.stack(kp_l), jnp.stack(vp_l), jnp.stack(ks_l), jnp.stack(vs_l),
            jnp.stack(pp_l), jnp.stack(ps_l))
```

```python
import functools

import jax
import jax.numpy as jnp
from jax import lax
from jax.experimental import pallas as pl
from jax.experimental.pallas import tpu as pltpu

N_HEADS = 16
HEAD_DIM = 64
POOL_WINDOWS = (2, 4, 8, 16)
POOL_STATE = max(POOL_WINDOWS) - 1
PAGE_SIZE = 128
RMS_EPS = 1e-6

F32 = jnp.float32
BF16 = jnp.bfloat16

VMEM_LIMIT_BYTES = 56 * 1024 * 1024
ATTN_BLOCK = 256
HEADS_PER_STEP = 2
PAGES_PER_STEP = 4
PAGES_PER_CHUNK = 2
Q_ROWS = 8
POOL_PREV_ROWS = 16

NT_DIMS = (((1,), (1,)), ((), ()))


def _rms(x, g):
    return x * lax.rsqrt(jnp.mean(x * x, axis=-1, keepdims=True) + RMS_EPS) * g


def _log_keep(z):
    return -(jnp.maximum(z, 0.0) + jnp.log1p(jnp.exp(-jnp.abs(z))))


def _suffix_matrix(n):
    rows = lax.broadcasted_iota(jnp.int32, (n, n), 0)
    cols = lax.broadcasted_iota(jnp.int32, (n, n), 1)
    return (rows >= cols).astype(BF16)


def _const_spec(shape):
    return pl.BlockSpec(shape, lambda *_: (0,) * len(shape), pipeline_mode=pl.Buffered(1))


def _project_kernel(feature_major, x_ref, g_ref, w_ref, wkv_t_ref,
                    q_ref, k_ref, v_ref, u_ref, ga_ref, gb_ref, kb_ref, vb_ref):
    d = x_ref.shape[2]
    xn = _rms(x_ref[0], g_ref[...]).astype(BF16)

    def col(i):
        return jnp.dot(xn, w_ref[:, i * d:(i + 1) * d], preferred_element_type=F32)

    def kv(i):
        w_t = wkv_t_ref[i * d:(i + 1) * d, :]
        if feature_major:
            return lax.dot_general(w_t, xn, NT_DIMS, preferred_element_type=F32)
        return lax.dot_general(xn, w_t, NT_DIMS, preferred_element_type=F32)

    q_ref[0] = (col(0) * (HEAD_DIM ** -0.5)).astype(BF16)
    k = kv(0)
    k_ref[0] = k
    kb_ref[0] = k.astype(BF16)
    v = kv(1)
    v_ref[0] = v
    vb_ref[0] = v.astype(BF16)
    u_ref[0] = col(1)
    ga_ref[0] = col(2)
    gb_ref[0] = col(3)


def _project(x, g, w_bf16, wkv_t_bf16, tm, feature_major):
    b, s, d = x.shape
    row_spec = pl.BlockSpec((1, tm, d), lambda bi, i: (bi, i, 0))
    if feature_major:
        kv_spec = pl.BlockSpec((1, d, tm), lambda bi, i: (bi, 0, i))
        kv_shape = (b, d, s)
    else:
        kv_spec, kv_shape = row_spec, (b, s, d)
    rows_f32 = jax.ShapeDtypeStruct((b, s, d), F32)
    return pl.pallas_call(
        functools.partial(_project_kernel, feature_major),
        grid=(b, s // tm),
        in_specs=[row_spec, _const_spec((1, d)), _const_spec(w_bf16.shape), _const_spec(wkv_t_bf16.shape)],
        out_specs=[row_spec, kv_spec, kv_spec, row_spec, row_spec, row_spec, kv_spec, kv_spec],
        out_shape=[jax.ShapeDtypeStruct((b, s, d), BF16),
                   jax.ShapeDtypeStruct(kv_shape, F32), jax.ShapeDtypeStruct(kv_shape, F32),
                   rows_f32, rows_f32, rows_f32,
                   jax.ShapeDtypeStruct(kv_shape, BF16), jax.ShapeDtypeStruct(kv_shape, BF16)],
        compiler_params=pltpu.CompilerParams(
            dimension_semantics=("arbitrary", "arbitrary"), vmem_limit_bytes=VMEM_LIMIT_BYTES),
        name="project",
    )(x, g, w_bf16, wkv_t_bf16)


def _prompt_attn_kernel(bias_ref, q_ref, kt_ref, vt_ref, o_ref):
    t = ATTN_BLOCK
    hp = pl.program_id(1)
    qi = pl.program_id(2)
    suffix = _suffix_matrix(t)
    rows = lax.broadcasted_iota(jnp.int32, (t, t), 0)
    cols = lax.broadcasted_iota(jnp.int32, (t, t), 1)
    causal = cols < rows

    for hh in range(HEADS_PER_STEP):
        feat = slice(hh * HEAD_DIM, (hh + 1) * HEAD_DIM)
        q = q_ref[0, :, feat]
        bias = bias_ref[hp * HEADS_PER_STEP + hh]

        def tile(kb, carry, masked):
            o, c = carry
            keys = pl.ds(pl.multiple_of(kb * t, t), t)
            z = jnp.dot(q, kt_ref[0, feat, keys], preferred_element_type=F32) + bias
            lk = _log_keep(z)
            if masked:
                lk = jnp.where(causal, lk, 0.0)
            incl = jnp.dot(lk.astype(BF16), suffix, preferred_element_type=F32) + c
            w = jnp.exp(z + incl)
            if masked:
                w = jnp.where(causal, w, 0.0)
            o = o + lax.dot_general(w.astype(BF16), vt_ref[0, feat, keys], NT_DIMS,
                                    preferred_element_type=F32)
            return o, incl[:, 0:1]

        carry = tile(qi, (jnp.zeros((t, HEAD_DIM), F32), jnp.zeros((t, 1), F32)), True)
        o, _ = lax.fori_loop(0, qi, lambda i, cr: tile(qi - 1 - i, cr, False), carry)
        o_ref[0, :, feat] = o


def _prompt_attention(qb, ktb, vtb, sb_bias):
    b, s, width = qb.shape
    t = ATTN_BLOCK
    lane_block = HEADS_PER_STEP * HEAD_DIM
    q_spec = pl.BlockSpec((1, t, lane_block), lambda bi, hp, qi: (bi, qi, hp))
    kv_spec = pl.BlockSpec((1, lane_block, s), lambda bi, hp, qi: (bi, hp, 0))
    return pl.pallas_call(
        _prompt_attn_kernel,
        grid=(b, width // lane_block, s // t),
        in_specs=[pl.BlockSpec(memory_space=pltpu.SMEM), q_spec, kv_spec, kv_spec],
        out_specs=q_spec,
        out_shape=jax.ShapeDtypeStruct((b, s, width), F32),
        compiler_params=pltpu.CompilerParams(
            dimension_semantics=("arbitrary", "arbitrary", "arbitrary"),
            vmem_limit_bytes=VMEM_LIMIT_BYTES),
        name="prompt_attention",
    )(sb_bias, qb, ktb, vtb)


def _sample_attn_kernel(pt_ref, q_ref, bias_ref, kn_ref, vn_ref, *refs):
    del pt_ref
    g = PAGES_PER_STEP
    k_refs, v_refs = refs[:g], refs[g:2 * g]
    o_ref, acc_ref, c_ref = refs[2 * g:]
    s = pl.program_id(1)
    hq, width = acc_ref.shape

    def chunk(kt_refs, vt_refs, own):
        n = PAGE_SIZE * len(kt_refs)
        kt = jnp.concatenate([r[0].astype(BF16) for r in kt_refs], axis=1)
        z = jnp.dot(q_ref[0], kt, preferred_element_type=F32) + bias_ref[...]
        lk = _log_keep(z)
        if own:
            qrow = lax.broadcasted_iota(jnp.int32, (hq, n), 0) % Q_ROWS
            readable = lax.broadcasted_iota(jnp.int32, (hq, n), 1) < qrow
            lk = jnp.where(readable, lk, 0.0)
        incl = jnp.dot(lk.astype(BF16), _suffix_matrix(n), preferred_element_type=F32) + c_ref[...]
        w = jnp.exp(z + incl)
        if own:
            w = jnp.where(readable, w, 0.0)
        c_ref[...] = incl[:, 0:1]
        vt = jnp.concatenate([r[0].astype(BF16) for r in vt_refs], axis=1)
        acc_ref[...] += lax.dot_general(w.astype(BF16), vt, NT_DIMS, preferred_element_type=F32)

    @pl.when(s == 0)
    def _():
        acc_ref[...] = jnp.zeros_like(acc_ref)
        c_ref[...] = jnp.zeros_like(c_ref)
        chunk([kn_ref], [vn_ref], True)

    for first in reversed(range(0, g, PAGES_PER_CHUNK)):
        pages = slice(first, first + PAGES_PER_CHUNK)
        chunk(k_refs[pages], v_refs[pages], False)

    @pl.when(s == pl.num_programs(1) - 1)
    def _():
        acc = acc_ref[...].reshape(N_HEADS, Q_ROWS, width)
        shape = (N_HEADS, Q_ROWS, width)
        own_head = (lax.broadcasted_iota(jnp.int32, shape, 0)
                    == lax.broadcasted_iota(jnp.int32, shape, 2) // HEAD_DIM)
        o_ref[0] = jnp.sum(jnp.where(own_head, acc, 0.0), axis=0)


def _sample_attention(q_heads, bias_col, kt_own, vt_own, cache_kt, cache_vt, page_table):
    db, n_pages = page_table.shape
    _, width, _ = cache_kt.shape
    hq = N_HEADS * Q_ROWS
    g = PAGES_PER_STEP
    n_steps = n_pages // g

    def page_spec(slot):
        return pl.BlockSpec(
            (1, width, PAGE_SIZE),
            lambda b, s, pt: (pt[b, (n_steps - 1 - s) * g + slot], 0, 0))

    own_spec = pl.BlockSpec((1, width, PAGE_SIZE), lambda b, s, pt: (b, 0, 0))
    grid_spec = pltpu.PrefetchScalarGridSpec(
        num_scalar_prefetch=1,
        grid=(db, n_steps),
        in_specs=[pl.BlockSpec((1, hq, width), lambda b, s, pt: (b, 0, 0)),
                  pl.BlockSpec((hq, 1), lambda b, s, pt: (0, 0)),
                  own_spec, own_spec]
                 + [page_spec(slot) for slot in range(g)] * 2,
        out_specs=pl.BlockSpec((1, Q_ROWS, width), lambda b, s, pt: (b, 0, 0)),
        scratch_shapes=[pltpu.VMEM((hq, width), F32), pltpu.VMEM((hq, 1), F32)],
    )
    return pl.pallas_call(
        _sample_attn_kernel,
        grid_spec=grid_spec,
        out_shape=jax.ShapeDtypeStruct((db, Q_ROWS, width), F32),
        compiler_params=pltpu.CompilerParams(
            dimension_semantics=("arbitrary", "arbitrary"), vmem_limit_bytes=VMEM_LIMIT_BYTES),
        name="sample_attention",
    )(page_table, q_heads, bias_col, kt_own, vt_own, *([cache_kt] * g), *([cache_vt] * g))


def _pool_core(load_rows, pos, wpool_ref, scale_ref, store):
    group = wpool_ref.shape[1]
    for g, w in enumerate(POOL_WINDOWS):
        cols = slice(g * group, (g + 1) * group)
        cur = load_rows(0, cols)
        total = cur
        for j in range(1, w):
            total = total + load_rows(j, cols)
        cnt = jnp.minimum(pos + 1, w).astype(F32)
        d = total * (1.0 / cnt) - cur
        y = jnp.dot(d.astype(BF16), wpool_ref[g], preferred_element_type=F32)
        store(cols, y * scale_ref[:, cols])


def _pool_prompt_kernel(prev_ref, u_ref, wpool_ref, scale_ref, o_ref, ext_ref):
    tm = u_ref.shape[1]
    i = pl.program_id(1)
    p = POOL_PREV_ROWS
    ext_ref[0:p, :] = jnp.where(i == 0, 0.0, prev_ref[0])
    ext_ref[p:p + tm, :] = u_ref[0]
    pos = i * tm + lax.broadcasted_iota(jnp.int32, (tm, 1), 0)

    def store(cols, y):
        o_ref[0, :, cols] = y

    _pool_core(lambda j, cols: ext_ref[pl.ds(p - j, tm), cols], pos, wpool_ref, scale_ref, store)


def _pool_prompt(u, wpool_bf16, scale, tm):
    b, s, c = u.shape
    p = POOL_PREV_ROWS
    return pl.pallas_call(
        _pool_prompt_kernel,
        grid=(b, s // tm),
        in_specs=[pl.BlockSpec((1, p, c), lambda bi, i: (bi, jnp.maximum(i * (tm // p) - 1, 0), 0)),
                  pl.BlockSpec((1, tm, c), lambda bi, i: (bi, i, 0)),
                  _const_spec(wpool_bf16.shape), _const_spec((1, c))],
        out_specs=pl.BlockSpec((1, tm, c), lambda bi, i: (bi, i, 0)),
        out_shape=jax.ShapeDtypeStruct((b, s, c), F32),
        scratch_shapes=[pltpu.VMEM((p + tm, c), F32)],
        compiler_params=pltpu.CompilerParams(
            dimension_semantics=("arbitrary", "arbitrary"), vmem_limit_bytes=VMEM_LIMIT_BYTES),
        name="pool_prompt",
    )(u, u, wpool_bf16, scale)


def _pool_sample_kernel(past, ext_ref, wpool_ref, scale_ref, o_ref):
    n_new = o_ref.shape[1]
    pos = past + lax.broadcasted_iota(jnp.int32, (n_new, 1), 0)

    def store(cols, y):
        o_ref[0, :, cols] = y

    _pool_core(lambda j, cols: ext_ref[0, pl.ds(POOL_STATE - j, n_new), cols],
               pos, wpool_ref, scale_ref, store)


def _pool_sample(u_ext, wpool_bf16, scale, past):
    db, ext_rows, c = u_ext.shape
    n_new = ext_rows - POOL_STATE
    return pl.pallas_call(
        functools.partial(_pool_sample_kernel, past),
        grid=(db,),
        in_specs=[pl.BlockSpec((1, ext_rows, c), lambda bi: (bi, 0, 0)),
                  _const_spec(wpool_bf16.shape), _const_spec((1, c))],
        out_specs=pl.BlockSpec((1, n_new, c), lambda bi: (bi, 0, 0)),
        out_shape=jax.ShapeDtypeStruct((db, n_new, c), F32),
        compiler_params=pltpu.CompilerParams(
            dimension_semantics=("arbitrary",), vmem_limit_bytes=VMEM_LIMIT_BYTES),
        name="pool_sample",
    )(u_ext, wpool_bf16, scale)


def _finish_kernel(x_ref, oa_ref, op_ref, ga_ref, gb_ref, p_ref,
                   g_mix_ref, g_pre_ref, g_post_ref,
                   w_out_ref, w_up_ref, w_down_ref, w_proj_ref, w_gate_ref, y_ref):
    merged = jax.nn.sigmoid(ga_ref[...]) * oa_ref[...] + jax.nn.sigmoid(gb_ref[...]) * op_ref[...]
    mix = jnp.dot(merged.astype(BF16), w_out_ref[...], preferred_element_type=F32)
    h = x_ref[...] + _rms(mix, g_mix_ref[...])
    up = jnp.dot(_rms(h, g_pre_ref[...]).astype(BF16), w_up_ref[...], preferred_element_type=F32)
    hid = jnp.square(jnp.maximum(up, 0.0)).astype(BF16)
    down = jnp.dot(hid, w_down_ref[...], preferred_element_type=F32)
    h = h + _rms(down, g_post_ref[...])
    gate = jax.nn.sigmoid(jnp.dot(h.astype(BF16), w_gate_ref[...], preferred_element_type=F32))
    ple = jnp.dot(p_ref[...].astype(BF16), w_proj_ref[...], preferred_element_type=F32)
    y_ref[...] = h + gate * ple


def _finish(x, o_att, o_pool, ga, gb, p, gains, weights, tm):
    rows, d = x.shape
    row_spec = pl.BlockSpec((tm, d), lambda i: (i, 0))
    p_spec = pl.BlockSpec((tm, p.shape[1]), lambda i: (i, 0))
    return pl.pallas_call(
        _finish_kernel,
        grid=(rows // tm,),
        in_specs=[row_spec] * 5 + [p_spec] + [_const_spec((1, d))] * 3
                 + [_const_spec(w.shape) for w in weights],
        out_specs=row_spec,
        out_shape=jax.ShapeDtypeStruct((rows, d), F32),
        compiler_params=pltpu.CompilerParams(
            dimension_semantics=("arbitrary",), vmem_limit_bytes=VMEM_LIMIT_BYTES),
        name="finish",
    )(x, o_att, o_pool, ga, gb, p, *gains, *weights)


def _feature_major_pages(cache):
    n_pool, page, heads, dim = cache.shape
    return cache.transpose(0, 2, 3, 1).reshape(n_pool, heads * dim, page)


def _token_major(x_t, b, s):
    return x_t.reshape(b, N_HEADS, HEAD_DIM, s).transpose(0, 3, 1, 2)[None]


def kernel(x_prompt, x_sample, cache_k, cache_v, state_pool, page_table, p_prompt, p_sample,
           norm_pre_mix, norm_post_mix, norm_pre_mlp, norm_post_mlp, w_in, sb_bias, w_pool, pool_scale,
           w_out, w_up, w_down, w_ple_proj, w_ple_gate):
    depth = w_in.shape[0]
    assert depth == 1, "single-layer step"
    b, s, d = x_prompt.shape
    db, dt, _ = x_sample.shape
    past = page_table.shape[1] * PAGE_SIZE
    l = 0

    w_in_b = w_in[l].astype(BF16)
    w_rows = jnp.concatenate([w_in_b[:, :d], w_in_b[:, 3 * d:]], axis=1)
    wkv_t = w_in_b[:, d:3 * d].T
    w_pool_b = w_pool[l].astype(BF16)
    weights = tuple(w[l].astype(BF16) for w in (w_out, w_up, w_down, w_ple_proj, w_ple_gate))
    gains = (norm_post_mix[l][None], norm_pre_mlp[l][None], norm_post_mlp[l][None])
    g_in = norm_pre_mix[l][None]
    scale = pool_scale[l][None]

    qb, kt, vt, u, ga, gb, ktb, vtb = _project(x_prompt, g_in, w_rows, wkv_t, 256, True)
    o_att = _prompt_attention(qb, ktb, vtb, sb_bias[l])
    o_pool = _pool_prompt(u, w_pool_b, scale, 256)
    rows_p = b * s
    y_prompt = _finish(x_prompt.reshape(rows_p, d), o_att.reshape(rows_p, d), o_pool.reshape(rows_p, d),
                       ga.reshape(rows_p, d), gb.reshape(rows_p, d), p_prompt[l].reshape(rows_p, -1),
                       gains, weights, 256).reshape(b, s, d)

    rows_s = db * dt
    xs = x_sample.reshape(1, rows_s, d)
    qs, ks, vs, us, gas, gbs, _, _ = _project(xs, g_in, w_rows, wkv_t, rows_s, False)
    ks, vs, us = (t.reshape(db, dt, d) for t in (ks, vs, us))
    q_pad = jnp.pad(qs.reshape(db, dt, N_HEADS, HEAD_DIM).transpose(0, 2, 1, 3),
                    ((0, 0), (0, 0), (0, Q_ROWS - dt), (0, 0)))
    q_heads = (q_pad[:, :, :, None, :] * jnp.eye(N_HEADS, dtype=BF16)[None, :, None, :, None]
               ).reshape(db, N_HEADS * Q_ROWS, d)
    bias_col = jnp.repeat(sb_bias[l], Q_ROWS)[:, None]

    def own_page(t):
        return jnp.pad(t.transpose(0, 2, 1), ((0, 0), (0, 0), (0, PAGE_SIZE - dt)))

    o_q = _sample_attention(q_heads, bias_col, own_page(ks), own_page(vs),
                            _feature_major_pages(cache_k[l]), _feature_major_pages(cache_v[l]), page_table)
    o_att_s = o_q[:, :dt].reshape(rows_s, d)
    u_ext = jnp.concatenate([state_pool[l], us], axis=1)
    o_pool_s = _pool_sample(u_ext, w_pool_b, scale, past)
    y_sample = _finish(xs.reshape(rows_s, d), o_att_s, o_pool_s.reshape(rows_s, d),
                       gas.reshape(rows_s, d), gbs.reshape(rows_s, d),
                       p_sample[l].reshape(rows_s, -1), gains, weights, rows_s).reshape(db, dt, d)

    heads = (N_HEADS, HEAD_DIM)
    return (y_prompt, y_sample, _token_major(kt, b, s), _token_major(vt, b, s),
            ks.reshape(1, db, dt, *heads), vs.reshape(1, db, dt, *heads),
            u[:, s - POOL_STATE:][None], u_ext[:, dt:][None])
```

```python
import functools

import jax
import jax.numpy as jnp
from jax import lax
from jax.experimental import pallas as pl
from jax.experimental.pallas import tpu as pltpu

N_HEADS = 16
HEAD_DIM = 64
POOL_WINDOWS = (2, 4, 8, 16)
POOL_STATE = max(POOL_WINDOWS) - 1
PAGE_SIZE = 128
RMS_EPS = 1e-6

F32 = jnp.float32
BF16 = jnp.bfloat16

VMEM_LIMIT_BYTES = 56 * 1024 * 1024
ATTN_BLOCK = 256
HEADS_PER_STEP = 8
HEADS_PER_DOT = 4
PAGES_PER_STEP = 8
PAGES_PER_CHUNK = 2
POOL_PREV_ROWS = 16

NT_DIMS = (((1,), (1,)), ((), ()))


def _rms(x, g):
    return x * lax.rsqrt(jnp.mean(x * x, axis=-1, keepdims=True) + RMS_EPS) * g


def _softplus(z):
    neg_abs = pltpu.bitcast(pltpu.bitcast(z, jnp.uint32) | jnp.uint32(0x80000000), F32)
    return jnp.maximum(z, 0.0) + jnp.log(1.0 + jnp.exp(neg_abs))


def _neg_suffix_matrix(n):
    rows = lax.broadcasted_iota(jnp.int32, (n, n), 0)
    cols = lax.broadcasted_iota(jnp.int32, (n, n), 1)
    return jnp.where(rows >= cols, -1.0, 0.0).astype(BF16)


def _const_spec(shape):
    return pl.BlockSpec(shape, lambda *_: (0,) * len(shape), pipeline_mode=pl.Buffered(1))


def _project_kernel(feature_major, x_ref, g_ref, w_ref, wt_ref,
                    q_ref, k_ref, v_ref, u_ref, ga_ref, gb_ref, kb_ref, vb_ref):
    d = x_ref.shape[2]
    xn = _rms(x_ref[0], g_ref[...]).astype(BF16)

    def col(i):
        return jnp.dot(xn, w_ref[:, i * d:(i + 1) * d], preferred_element_type=F32)

    def qkv(i):
        w_t = wt_ref[i * d:(i + 1) * d, :]
        if feature_major:
            return lax.dot_general(w_t, xn, NT_DIMS, preferred_element_type=F32)
        return lax.dot_general(xn, w_t, NT_DIMS, preferred_element_type=F32)

    q_ref[0] = (qkv(0) * (HEAD_DIM ** -0.5)).astype(BF16)
    k = qkv(1)
    k_ref[0] = k
    kb_ref[0] = (col(0) if feature_major else k).astype(BF16)
    v = qkv(2)
    v_ref[0] = v
    vb_ref[0] = v.astype(BF16)
    u_ref[0] = col(1)
    ga_ref[0] = col(2)
    gb_ref[0] = col(3)


def _project(x, g, w_bf16, wt_bf16, tm, feature_major):
    b, s, d = x.shape
    row_spec = pl.BlockSpec((1, tm, d), lambda bi, i: (bi, i, 0))
    if feature_major:
        fm_spec = pl.BlockSpec((1, d, tm), lambda bi, i: (bi, 0, i))
        fm_shape = (b, d, s)
    else:
        fm_spec, fm_shape = row_spec, (b, s, d)
    rows_f32 = jax.ShapeDtypeStruct((b, s, d), F32)
    return pl.pallas_call(
        functools.partial(_project_kernel, feature_major),
        grid=(b, s // tm),
        in_specs=[row_spec, _const_spec((1, d)), _const_spec(w_bf16.shape), _const_spec(wt_bf16.shape)],
        out_specs=[fm_spec, fm_spec, fm_spec, row_spec, row_spec, row_spec, row_spec, fm_spec],
        out_shape=[jax.ShapeDtypeStruct(fm_shape, BF16),
                   jax.ShapeDtypeStruct(fm_shape, F32), jax.ShapeDtypeStruct(fm_shape, F32),
                   rows_f32, rows_f32, rows_f32,
                   jax.ShapeDtypeStruct((b, s, d), BF16), jax.ShapeDtypeStruct(fm_shape, BF16)],
        compiler_params=pltpu.CompilerParams(
            dimension_semantics=("arbitrary", "arbitrary"), vmem_limit_bytes=VMEM_LIMIT_BYTES),
        name="project",
    )(x, g, w_bf16, wt_bf16)


def _prompt_attn_kernel(bias_ref, qt_ref, k_ref, vt_ref, o_ref):
    t = ATTN_BLOCK
    hg = pl.program_id(1)
    qi = pl.program_id(2)
    rows = lax.broadcasted_iota(jnp.int32, (t, t), 0)
    cols = lax.broadcasted_iota(jnp.int32, (t, t), 1)
    neg_suffix = jnp.where(cols >= rows, -1.0, 0.0).astype(BF16)
    causal = rows < cols
    feats = [slice(hh * HEAD_DIM, (hh + 1) * HEAD_DIM) for hh in range(HEADS_PER_STEP)]
    biases = [bias_ref[hg * HEADS_PER_STEP + hh] for hh in range(HEADS_PER_STEP)]
    depth = HEADS_PER_DOT * HEAD_DIM
    group_feats = [slice(hh // HEADS_PER_DOT * depth, (hh // HEADS_PER_DOT + 1) * depth)
                   for hh in range(HEADS_PER_STEP)]
    feat_head = lax.broadcasted_iota(jnp.int32, (depth, t), 0) // HEAD_DIM
    qts = [jnp.where(feat_head == hh % HEADS_PER_DOT, qt_ref[0, group_feats[hh], :].astype(F32), 0.0
                     ).astype(BF16) for hh in range(HEADS_PER_STEP)]

    def tile(kb, carry, masked):
        keys = pl.ds(pl.multiple_of(kb * t, t), t)
        heads = range(HEADS_PER_STEP)
        zs = [jnp.dot(k_ref[0, keys, group_feats[hh]], qts[hh], preferred_element_type=F32) + biases[hh]
              for hh in heads]
        sps = [_softplus(z) for z in zs]
        if masked:
            sps = [jnp.where(causal, sp, 0.0) for sp in sps]
        incls = [jnp.dot(neg_suffix, sps[hh].astype(BF16), preferred_element_type=F32) + carry[hh][1]
                 for hh in heads]
        ws = [jnp.exp(zs[hh] + incls[hh]) for hh in heads]
        if masked:
            ws = [jnp.where(causal, w, 0.0) for w in ws]
        return tuple(
            (carry[hh][0] + jnp.dot(vt_ref[0, feats[hh], keys], ws[hh].astype(BF16),
                                    preferred_element_type=F32),
             incls[hh][0:1, :])
            for hh in heads)

    zero = (jnp.zeros((HEAD_DIM, t), F32), jnp.zeros((1, t), F32))
    carry = tile(qi, (zero,) * HEADS_PER_STEP, True)
    carry = lax.fori_loop(0, qi, lambda i, cr: tile(qi - 1 - i, cr, False), carry)
    o_ref[0] = jnp.concatenate([o_t for o_t, _ in carry], axis=0).T


def _prompt_attention(qt, kb, vtb, sb_bias):
    b, s, d = kb.shape
    t = ATTN_BLOCK
    width = HEADS_PER_STEP * HEAD_DIM
    return pl.pallas_call(
        _prompt_attn_kernel,
        grid=(b, d // width, s // t),
        in_specs=[pl.BlockSpec(memory_space=pltpu.SMEM),
                  pl.BlockSpec((1, width, t), lambda bi, hg, qi: (bi, hg, qi)),
                  pl.BlockSpec((1, s, width), lambda bi, hg, qi: (bi, 0, hg)),
                  pl.BlockSpec((1, width, s), lambda bi, hg, qi: (bi, hg, 0))],
        out_specs=pl.BlockSpec((1, t, width), lambda bi, hg, qi: (bi, qi, hg)),
        out_shape=jax.ShapeDtypeStruct((b, s, d), F32),
        compiler_params=pltpu.CompilerParams(
            dimension_semantics=("arbitrary", "arbitrary", "arbitrary"),
            vmem_limit_bytes=VMEM_LIMIT_BYTES),
        name="prompt_attention",
    )(sb_bias, qt, kb, vtb)


def _sample_attn_kernel(pt_ref, q_ref, bias_ref, kn_ref, vn_ref, *refs):
    del pt_ref
    g = PAGES_PER_STEP
    k_refs, v_refs = refs[:g], refs[g:2 * g]
    o_ref, acc_ref, c_ref = refs[2 * g:]
    s = pl.program_id(1)
    hq, width = acc_ref.shape

    def sweep(kt_refs, vt_refs, own):
        per_chunk = min(PAGES_PER_CHUNK, len(kt_refs))
        n = PAGE_SIZE * per_chunk
        n_chunks = len(kt_refs) // per_chunk
        kt = jnp.concatenate([r[0].astype(BF16) for r in kt_refs], axis=1)
        z = jnp.dot(q_ref[0], kt, preferred_element_type=F32) + bias_ref[...]
        sp = _softplus(z)
        if own:
            token = lax.broadcasted_iota(jnp.int32, z.shape, 0) // N_HEADS
            readable = lax.broadcasted_iota(jnp.int32, z.shape, 1) < token
            sp = jnp.where(readable, sp, 0.0)
        neg_suffix = _neg_suffix_matrix(n)
        incls = [jnp.dot(sp[:, i * n:(i + 1) * n].astype(BF16), neg_suffix, preferred_element_type=F32)
                 for i in range(n_chunks)]
        c = c_ref[...]
        for i in reversed(range(n_chunks)):
            incls[i] = incls[i] + c
            c = incls[i][:, 0:1]
        c_ref[...] = c
        w = jnp.exp(z + jnp.concatenate(incls, axis=1))
        if own:
            w = jnp.where(readable, w, 0.0)
        vt = jnp.concatenate([r[0].astype(BF16) for r in vt_refs], axis=1)
        acc_ref[...] += lax.dot_general(w.astype(BF16), vt, NT_DIMS, preferred_element_type=F32)

    @pl.when(s == 0)
    def _():
        acc_ref[...] = jnp.zeros_like(acc_ref)
        c_ref[...] = jnp.zeros_like(c_ref)
        sweep([kn_ref], [vn_ref], True)

    sweep(k_refs, v_refs, False)

    @pl.when(s == pl.num_programs(1) - 1)
    def _():
        shape = (N_HEADS, width)
        own_head = (lax.broadcasted_iota(jnp.int32, shape, 0)
                    == lax.broadcasted_iota(jnp.int32, shape, 1) // HEAD_DIM)
        for t in range(o_ref.shape[1]):
            rows = acc_ref[t * N_HEADS:(t + 1) * N_HEADS, :]
            o_ref[0, t:t + 1, :] = jnp.sum(jnp.where(own_head, rows, 0.0), axis=0, keepdims=True)


def _sample_attention(q_heads, bias_col, kt_own, vt_own, cache_kt, cache_vt, page_table):
    db, n_pages = page_table.shape
    _, hq, width = q_heads.shape
    g = PAGES_PER_STEP
    n_steps = n_pages // g

    def page_spec(slot):
        return pl.BlockSpec(
            (1, width, PAGE_SIZE),
            lambda b, s, pt: (pt[b, (n_steps - 1 - s) * g + slot], 0, 0))

    own_spec = pl.BlockSpec((1, width, PAGE_SIZE), lambda b, s, pt: (b, 0, 0))
    grid_spec = pltpu.PrefetchScalarGridSpec(
        num_scalar_prefetch=1,
        grid=(db, n_steps),
        in_specs=[pl.BlockSpec((1, hq, width), lambda b, s, pt: (b, 0, 0)),
                  pl.BlockSpec((hq, 1), lambda b, s, pt: (0, 0)),
                  own_spec, own_spec]
                 + [page_spec(slot) for slot in range(g)] * 2,
        out_specs=pl.BlockSpec((1, hq // N_HEADS, width), lambda b, s, pt: (b, 0, 0)),
        scratch_shapes=[pltpu.VMEM((hq, width), F32), pltpu.VMEM((hq, 1), F32)],
    )
    return pl.pallas_call(
        _sample_attn_kernel,
        grid_spec=grid_spec,
        out_shape=jax.ShapeDtypeStruct((db, hq // N_HEADS, width), F32),
        compiler_params=pltpu.CompilerParams(
            dimension_semantics=("arbitrary", "arbitrary"), vmem_limit_bytes=VMEM_LIMIT_BYTES),
        name="sample_attention",
    )(page_table, q_heads, bias_col, kt_own, vt_own, *([cache_kt] * g), *([cache_vt] * g))


def _pool_core(load_rows, pos, wpool_ref, scale_ref, store):
    group = wpool_ref.shape[1]
    for g, w in enumerate(POOL_WINDOWS):
        cols = slice(g * group, (g + 1) * group)
        cur = load_rows(0, cols)
        total = cur
        for j in range(1, w):
            total = total + load_rows(j, cols)
        cnt = jnp.minimum(pos + 1, w).astype(F32)
        d = total * (1.0 / cnt) - cur
        y = jnp.dot(d.astype(BF16), wpool_ref[g], preferred_element_type=F32)
        store(cols, y * scale_ref[:, cols])


def _pool_prompt_kernel(prev_ref, u_ref, wpool_ref, scale_ref, o_ref, ext_ref):
    tm = u_ref.shape[1]
    i = pl.program_id(1)
    p = POOL_PREV_ROWS
    ext_ref[0:p, :] = jnp.where(i == 0, 0.0, prev_ref[0])
    ext_ref[p:p + tm, :] = u_ref[0]
    pos = i * tm + lax.broadcasted_iota(jnp.int32, (tm, 1), 0)

    def store(cols, y):
        o_ref[0, :, cols] = y

    _pool_core(lambda j, cols: ext_ref[pl.ds(p - j, tm), cols], pos, wpool_ref, scale_ref, store)


def _pool_prompt(u, wpool_bf16, scale, tm):
    b, s, c = u.shape
    p = POOL_PREV_ROWS
    return pl.pallas_call(
        _pool_prompt_kernel,
        grid=(b, s // tm),
        in_specs=[pl.BlockSpec((1, p, c), lambda bi, i: (bi, jnp.maximum(i * (tm // p) - 1, 0), 0)),
                  pl.BlockSpec((1, tm, c), lambda bi, i: (bi, i, 0)),
                  _const_spec(wpool_bf16.shape), _const_spec((1, c))],
        out_specs=pl.BlockSpec((1, tm, c), lambda bi, i: (bi, i, 0)),
        out_shape=jax.ShapeDtypeStruct((b, s, c), F32),
        scratch_shapes=[pltpu.VMEM((p + tm, c), F32)],
        compiler_params=pltpu.CompilerParams(
            dimension_semantics=("arbitrary", "arbitrary"), vmem_limit_bytes=VMEM_LIMIT_BYTES),
        name="pool_prompt",
    )(u, u, wpool_bf16, scale)


def _pool_sample_kernel(past, ext_ref, wpool_ref, scale_ref, o_ref):
    n_new = o_ref.shape[1]
    pos = past + lax.broadcasted_iota(jnp.int32, (n_new, 1), 0)

    def store(cols, y):
        o_ref[0, :, cols] = y

    _pool_core(lambda j, cols: ext_ref[0, pl.ds(POOL_STATE - j, n_new), cols],
               pos, wpool_ref, scale_ref, store)


def _pool_sample(u_ext, wpool_bf16, scale, past):
    db, ext_rows, c = u_ext.shape
    n_new = ext_rows - POOL_STATE
    return pl.pallas_call(
        functools.partial(_pool_sample_kernel, past),
        grid=(db,),
        in_specs=[pl.BlockSpec((1, ext_rows, c), lambda bi: (bi, 0, 0)),
                  _const_spec(wpool_bf16.shape), _const_spec((1, c))],
        out_specs=pl.BlockSpec((1, n_new, c), lambda bi: (bi, 0, 0)),
        out_shape=jax.ShapeDtypeStruct((db, n_new, c), F32),
        compiler_params=pltpu.CompilerParams(
            dimension_semantics=("arbitrary",), vmem_limit_bytes=VMEM_LIMIT_BYTES),
        name="pool_sample",
    )(u_ext, wpool_bf16, scale)


def _finish_kernel(x_ref, oa_ref, op_ref, ga_ref, gb_ref, p_ref,
                   g_mix_ref, g_pre_ref, g_post_ref,
                   w_out_ref, w_up_ref, w_down_ref, w_proj_ref, w_gate_ref, y_ref):
    merged = jax.nn.sigmoid(ga_ref[...]) * oa_ref[...] + jax.nn.sigmoid(gb_ref[...]) * op_ref[...]
    mix = jnp.dot(merged.astype(BF16), w_out_ref[...], preferred_element_type=F32)
    h = x_ref[...] + _rms(mix, g_mix_ref[...])
    up = jnp.dot(_rms(h, g_pre_ref[...]).astype(BF16), w_up_ref[...], preferred_element_type=F32)
    hid = jnp.square(jnp.maximum(up, 0.0)).astype(BF16)
    down = jnp.dot(hid, w_down_ref[...], preferred_element_type=F32)
    h = h + _rms(down, g_post_ref[...])
    gate = jax.nn.sigmoid(jnp.dot(h.astype(BF16), w_gate_ref[...], preferred_element_type=F32))
    ple = jnp.dot(p_ref[...].astype(BF16), w_proj_ref[...], preferred_element_type=F32)
    y_ref[...] = h + gate * ple


def _finish(x, o_att, o_pool, ga, gb, p, gains, weights, tm):
    rows, d = x.shape
    row_spec = pl.BlockSpec((tm, d), lambda i: (i, 0))
    p_spec = pl.BlockSpec((tm, p.shape[1]), lambda i: (i, 0))
    return pl.pallas_call(
        _finish_kernel,
        grid=(rows // tm,),
        in_specs=[row_spec] * 5 + [p_spec] + [_const_spec((1, d))] * 3
                 + [_const_spec(w.shape) for w in weights],
        out_specs=row_spec,
        out_shape=jax.ShapeDtypeStruct((rows, d), F32),
        compiler_params=pltpu.CompilerParams(
            dimension_semantics=("arbitrary",), vmem_limit_bytes=VMEM_LIMIT_BYTES),
        name="finish",
    )(x, o_att, o_pool, ga, gb, p, *gains, *weights)


def _feature_major_pages(cache):
    n_pool, page, heads, dim = cache.shape
    return cache.transpose(0, 2, 3, 1).reshape(n_pool, heads * dim, page)


def _token_major(x_t, b, s):
    return x_t.reshape(b, N_HEADS, HEAD_DIM, s).transpose(0, 3, 1, 2)[None]


def kernel(x_prompt, x_sample, cache_k, cache_v, state_pool, page_table, p_prompt, p_sample,
           norm_pre_mix, norm_post_mix, norm_pre_mlp, norm_post_mlp, w_in, sb_bias, w_pool, pool_scale,
           w_out, w_up, w_down, w_ple_proj, w_ple_gate):
    depth = w_in.shape[0]
    assert depth == 1, "single-layer step"
    b, s, d = x_prompt.shape
    db, dt, _ = x_sample.shape
    past = page_table.shape[1] * PAGE_SIZE
    l = 0

    w_in_b = w_in[l].astype(BF16)
    w_rows = jnp.concatenate([w_in_b[:, d:2 * d], w_in_b[:, 3 * d:]], axis=1)
    w_t = w_in_b[:, :3 * d].T
    w_pool_b = w_pool[l].astype(BF16)
    weights = tuple(w[l].astype(BF16) for w in (w_out, w_up, w_down, w_ple_proj, w_ple_gate))
    gains = (norm_post_mix[l][None], norm_pre_mlp[l][None], norm_post_mlp[l][None])
    g_in = norm_pre_mix[l][None]
    scale = pool_scale[l][None]

    qt, kt, vt, u, ga, gb, kb, vtb = _project(x_prompt, g_in, w_rows, w_t, 256, True)
    o_att = _prompt_attention(qt, kb, vtb, sb_bias[l])
    o_pool = _pool_prompt(u, w_pool_b, scale, 256)
    rows_p = b * s
    y_prompt = _finish(x_prompt.reshape(rows_p, d), o_att.reshape(rows_p, d), o_pool.reshape(rows_p, d),
                       ga.reshape(rows_p, d), gb.reshape(rows_p, d), p_prompt[l].reshape(rows_p, -1),
                       gains, weights, 256).reshape(b, s, d)

    rows_s = db * dt
    xs = x_sample.reshape(1, rows_s, d)
    qs, ks, vs, us, gas, gbs, _, _ = _project(xs, g_in, w_rows, w_t, rows_s, False)
    ks, vs, us = (t.reshape(db, dt, d) for t in (ks, vs, us))
    q_heads = (qs.reshape(db, dt, N_HEADS, 1, HEAD_DIM) * jnp.eye(N_HEADS, dtype=BF16)[None, None, :, :, None]
               ).reshape(db, dt * N_HEADS, d)
    bias_col = jnp.tile(sb_bias[l], dt)[:, None]

    def own_page(t):
        return jnp.pad(t.transpose(0, 2, 1), ((0, 0), (0, 0), (0, PAGE_SIZE - dt)))

    o_q = _sample_attention(q_heads, bias_col, own_page(ks), own_page(vs),
                            _feature_major_pages(cache_k[l]), _feature_major_pages(cache_v[l]), page_table)
    o_att_s = o_q.reshape(rows_s, d)
    u_ext = jnp.concatenate([state_pool[l], us], axis=1)
    o_pool_s = _pool_sample(u_ext, w_pool_b, scale, past)
    y_sample = _finish(xs.reshape(rows_s, d), o_att_s, o_pool_s.reshape(rows_s, d),
                       gas.reshape(rows_s, d), gbs.reshape(rows_s, d),
                       p_sample[l].reshape(rows_s, -1), gains, weights, rows_s).reshape(db, dt, d)

    heads = (N_HEADS, HEAD_DIM)
    return (y_prompt, y_sample, _token_major(kt, b, s), _token_major(vt, b, s),
            ks.reshape(1, db, dt, *heads), vs.reshape(1, db, dt, *heads),
            u[:, s - POOL_STATE:][None], u_ext[:, dt:][None])
```

```python
import functools

import jax
import jax.numpy as jnp
from jax import lax
from jax.experimental import pallas as pl
from jax.experimental.pallas import tpu as pltpu

N_HEADS = 16
HEAD_DIM = 64
POOL_WINDOWS = (2, 4, 8, 16)
POOL_STATE = max(POOL_WINDOWS) - 1
PAGE_SIZE = 128
RMS_EPS = 1e-6

F32 = jnp.float32
BF16 = jnp.bfloat16

VMEM_LIMIT_BYTES = 56 * 1024 * 1024
ATTN_BLOCK = 256
HEADS_PER_STEP = 8
HEADS_PER_DOT = 4
PAGES_PER_STEP = 4
PAGES_PER_CHUNK = 2
POOL_PREV_ROWS = 16

NT_DIMS = (((1,), (1,)), ((), ()))


def _rms(x, g):
    return x * lax.rsqrt(jnp.mean(x * x, axis=-1, keepdims=True) + RMS_EPS) * g


def _softplus(z):
    neg_abs = pltpu.bitcast(pltpu.bitcast(z, jnp.uint32) | jnp.uint32(0x80000000), F32)
    return jnp.maximum(z, 0.0) + jnp.log(1.0 + jnp.exp(neg_abs))


def _neg_suffix_matrix(n):
    rows = lax.broadcasted_iota(jnp.int32, (n, n), 0)
    cols = lax.broadcasted_iota(jnp.int32, (n, n), 1)
    return jnp.where(rows >= cols, -1.0, 0.0).astype(BF16)


def _const_spec(shape):
    return pl.BlockSpec(shape, lambda *_: (0,) * len(shape), pipeline_mode=pl.Buffered(1))


def _project_kernel(feature_major, x_ref, g_ref, w_ref, wt_ref,
                    q_ref, k_ref, v_ref, u_ref, ga_ref, gb_ref, kb_ref, vb_ref):
    d = x_ref.shape[2]
    xn = _rms(x_ref[0], g_ref[...]).astype(BF16)

    def col(i):
        return jnp.dot(xn, w_ref[:, i * d:(i + 1) * d], preferred_element_type=F32)

    def qkv(i):
        w_t = wt_ref[i * d:(i + 1) * d, :]
        if feature_major:
            return lax.dot_general(w_t, xn, NT_DIMS, preferred_element_type=F32)
        return lax.dot_general(xn, w_t, NT_DIMS, preferred_element_type=F32)

    q_ref[0] = (qkv(0) * (HEAD_DIM ** -0.5)).astype(BF16)
    k = qkv(1)
    k_ref[0] = k
    kb_ref[0] = (col(0) if feature_major else k).astype(BF16)
    v = qkv(2)
    v_ref[0] = v
    vb_ref[0] = v.astype(BF16)
    u_ref[0] = col(1)
    ga_ref[0] = col(2)
    gb_ref[0] = col(3)


def _project(x, g, w_bf16, wt_bf16, tm, feature_major):
    b, s, d = x.shape
    row_spec = pl.BlockSpec((1, tm, d), lambda bi, i: (bi, i, 0))
    if feature_major:
        fm_spec = pl.BlockSpec((1, d, tm), lambda bi, i: (bi, 0, i))
        fm_shape = (b, d, s)
    else:
        fm_spec, fm_shape = row_spec, (b, s, d)
    rows_f32 = jax.ShapeDtypeStruct((b, s, d), F32)
    return pl.pallas_call(
        functools.partial(_project_kernel, feature_major),
        grid=(b, s // tm),
        in_specs=[row_spec, _const_spec((1, d)), _const_spec(w_bf16.shape), _const_spec(wt_bf16.shape)],
        out_specs=[fm_spec, fm_spec, fm_spec, row_spec, row_spec, row_spec, row_spec, fm_spec],
        out_shape=[jax.ShapeDtypeStruct(fm_shape, BF16),
                   jax.ShapeDtypeStruct(fm_shape, F32), jax.ShapeDtypeStruct(fm_shape, F32),
                   rows_f32, rows_f32, rows_f32,
                   jax.ShapeDtypeStruct((b, s, d), BF16), jax.ShapeDtypeStruct(fm_shape, BF16)],
        compiler_params=pltpu.CompilerParams(
            dimension_semantics=("arbitrary", "arbitrary"), vmem_limit_bytes=VMEM_LIMIT_BYTES),
        name="project",
    )(x, g, w_bf16, wt_bf16)


def _attention_kernel(n_groups, n_live, pt_ref, qi_ref, kb_ref,
                      bias_ref, qt_ref, k_ref, vt_ref,
                      qs_ref, bias_col_ref, kn_ref, vn_ref, *refs):
    del pt_ref
    g = PAGES_PER_STEP
    kp_refs, vp_refs = refs[:g], refs[g:2 * g]
    o_ref, os_ref, q_scr, o_acc, c_acc, s_acc, s_c = refs[2 * g:]
    t = ATTN_BLOCK
    hg = pl.program_id(1)
    st = pl.program_id(2)
    qi = qi_ref[st]
    kb = kb_ref[st]
    it = (pl.program_id(0) * pl.num_programs(1) + hg) * pl.num_programs(2) + st
    group = it % n_groups
    live = it < n_live

    rows = lax.broadcasted_iota(jnp.int32, (t, t), 0)
    cols = lax.broadcasted_iota(jnp.int32, (t, t), 1)
    neg_suffix_t = jnp.where(cols >= rows, -1.0, 0.0).astype(BF16)
    causal = rows < cols
    heads = range(HEADS_PER_STEP)
    feats = [slice(hh * HEAD_DIM, (hh + 1) * HEAD_DIM) for hh in heads]
    biases = [bias_ref[hg * HEADS_PER_STEP + hh] for hh in heads]
    depth = HEADS_PER_DOT * HEAD_DIM
    group_feats = [slice(hh // HEADS_PER_DOT * depth, (hh // HEADS_PER_DOT + 1) * depth) for hh in heads]

    def head_queries(hh):
        feat_head = lax.broadcasted_iota(jnp.int32, (depth, t), 0) // HEAD_DIM
        return jnp.where(feat_head == hh % HEADS_PER_DOT, qt_ref[0, group_feats[hh], :].astype(F32), 0.0
                         ).astype(BF16)

    def sample_logits(kt_refs, own):
        kt = jnp.concatenate([r[0].astype(BF16) for r in kt_refs], axis=1)
        z = jnp.dot(qs_ref[0], kt, preferred_element_type=F32) + bias_col_ref[...]
        readable = None
        if own:
            token = lax.broadcasted_iota(jnp.int32, z.shape, 0) // N_HEADS
            readable = lax.broadcasted_iota(jnp.int32, z.shape, 1) < token
        return z, readable

    def sample_suffix(sp, n_pages):
        n = PAGE_SIZE * min(PAGES_PER_CHUNK, n_pages)
        neg_suffix = _neg_suffix_matrix(n)
        incls = [jnp.dot(sp[:, i:i + n].astype(BF16), neg_suffix, preferred_element_type=F32)
                 for i in range(0, PAGE_SIZE * n_pages, n)]
        c = s_c[...]
        for i in reversed(range(len(incls))):
            incls[i] = incls[i] + c
            c = incls[i][:, 0:1]
        s_c[...] = c
        return jnp.concatenate(incls, axis=1)

    def sample_values(w, vt_refs):
        vt = jnp.concatenate([r[0].astype(BF16) for r in vt_refs], axis=1)
        s_acc[...] += lax.dot_general(w.astype(BF16), vt, NT_DIMS, preferred_element_type=F32)

    @pl.when(jnp.logical_and(live, group == 0))
    def _():
        s_acc[...] = jnp.zeros_like(s_acc)
        s_c[...] = jnp.zeros_like(s_c)
        z, readable = sample_logits([kn_ref], True)
        incl = sample_suffix(jnp.where(readable, _softplus(z), 0.0), 1)
        sample_values(jnp.where(readable, jnp.exp(z + incl), 0.0), [vn_ref])

    def step(masked):
        if masked:
            for hh in heads:
                q_scr[hh] = head_queries(hh)
        zs = [jnp.dot(k_ref[0, :, group_feats[hh]], q_scr[hh], preferred_element_type=F32) + biases[hh]
              for hh in heads]
        z_s, _ = sample_logits(kp_refs, False)
        sps = [_softplus(z) for z in zs]
        if masked:
            sps = [jnp.where(causal, sp, 0.0) for sp in sps]
        sp_s = _softplus(z_s)
        incls = [jnp.dot(neg_suffix_t, sps[hh].astype(BF16), preferred_element_type=F32) for hh in heads]
        if not masked:
            incls = [incls[hh] + c_acc[hh:hh + 1, :] for hh in heads]
        incl_s = sample_suffix(sp_s, g)
        ws = [jnp.exp(zs[hh] + incls[hh]) for hh in heads]
        if masked:
            ws = [jnp.where(causal, w, 0.0) for w in ws]
        w_s = jnp.exp(z_s + incl_s)
        for hh in heads:
            o_t = jnp.dot(vt_ref[0, feats[hh], :], ws[hh].astype(BF16), preferred_element_type=F32)
            if masked:
                o_acc[feats[hh], :] = o_t
            else:
                o_acc[feats[hh], :] += o_t
            c_acc[hh:hh + 1, :] = incls[hh][0:1, :]
        sample_values(w_s, vp_refs)

    pl.when(kb == qi)(lambda: step(True))
    pl.when(kb != qi)(lambda: step(False))

    @pl.when(kb == 0)
    def _():
        o_ref[0] = o_acc[...].T

    @pl.when(jnp.logical_and(live, group == n_groups - 1))
    def _():
        width = s_acc.shape[1]
        shape = (N_HEADS, width)
        own_head = (lax.broadcasted_iota(jnp.int32, shape, 0)
                    == lax.broadcasted_iota(jnp.int32, shape, 1) // HEAD_DIM)
        for tok in range(os_ref.shape[1]):
            blk = s_acc[tok * N_HEADS:(tok + 1) * N_HEADS, :]
            os_ref[0, tok:tok + 1, :] = jnp.sum(jnp.where(own_head, blk, 0.0), axis=0, keepdims=True)


def _attention(qt, kb, vtb, sb_bias, q_heads, bias_col, kt_own, vt_own, cache_kt, cache_vt, page_table):
    b, s, d = kb.shape
    db, n_pages = page_table.shape
    _, hq, _ = q_heads.shape
    t = ATTN_BLOCK
    g = PAGES_PER_STEP
    width = HEADS_PER_STEP * HEAD_DIM
    n_hg = d // width
    n_groups = n_pages // g
    n_live = db * n_groups
    tiles = [(qi, kb_) for qi in range(s // t) for kb_ in range(qi, -1, -1)]
    n_st = len(tiles)
    assert b * n_hg * n_st >= n_live, "not enough prompt tiles to carry the page stream"
    qi_tab = jnp.asarray([qi for qi, _ in tiles], jnp.int32)
    kb_tab = jnp.asarray([kb_ for _, kb_ in tiles], jnp.int32)

    def stream(bi, hg, st):
        it = jnp.minimum((bi * n_hg + hg) * n_st + st, n_live - 1)
        return it // n_groups, it % n_groups

    def seq_spec(shape):
        return pl.BlockSpec(shape, lambda bi, hg, st, pt, qi_t, kb_t: (stream(bi, hg, st)[0], 0, 0))

    def page_spec(slot):
        def index(bi, hg, st, pt, qi_t, kb_t):
            seq, grp = stream(bi, hg, st)
            return pt[seq, (n_groups - 1 - grp) * g + slot], 0, 0
        return pl.BlockSpec((1, d, PAGE_SIZE), index)

    grid_spec = pltpu.PrefetchScalarGridSpec(
        num_scalar_prefetch=3,
        grid=(b, n_hg, n_st),
        in_specs=[pl.BlockSpec(memory_space=pltpu.SMEM),
                  pl.BlockSpec((1, width, t), lambda bi, hg, st, pt, qi_t, kb_t: (bi, hg, qi_t[st])),
                  pl.BlockSpec((1, t, width), lambda bi, hg, st, pt, qi_t, kb_t: (bi, kb_t[st], hg)),
                  pl.BlockSpec((1, width, t), lambda bi, hg, st, pt, qi_t, kb_t: (bi, hg, kb_t[st])),
                  seq_spec((1, hq, d)),
                  pl.BlockSpec((hq, 1), lambda *_: (0, 0)),
                  seq_spec((1, d, PAGE_SIZE)), seq_spec((1, d, PAGE_SIZE))]
                 + [page_spec(slot) for slot in range(g)] * 2,
        out_specs=[pl.BlockSpec((1, t, width), lambda bi, hg, st, pt, qi_t, kb_t: (bi, qi_t[st], hg)),
                   seq_spec((1, hq // N_HEADS, d))],
        scratch_shapes=[pltpu.VMEM((HEADS_PER_STEP, HEADS_PER_DOT * HEAD_DIM, t), BF16),
                        pltpu.VMEM((width, t), F32), pltpu.VMEM((HEADS_PER_STEP, t), F32),
                        pltpu.VMEM((hq, d), F32), pltpu.VMEM((hq, 1), F32)],
    )
    return pl.pallas_call(
        functools.partial(_attention_kernel, n_groups, n_live),
        grid_spec=grid_spec,
        out_shape=[jax.ShapeDtypeStruct((b, s, d), F32),
                   jax.ShapeDtypeStruct((db, hq // N_HEADS, d), F32)],
        compiler_params=pltpu.CompilerParams(
            dimension_semantics=("arbitrary", "arbitrary", "arbitrary"),
            vmem_limit_bytes=VMEM_LIMIT_BYTES),
        name="attention",
    )(page_table, qi_tab, kb_tab, sb_bias, qt, kb, vtb, q_heads, bias_col, kt_own, vt_own,
      *([cache_kt] * g), *([cache_vt] * g))


def _pool_core(load_rows, pos, wpool_ref, scale_ref, store):
    group = wpool_ref.shape[1]
    for g, w in enumerate(POOL_WINDOWS):
        cols = slice(g * group, (g + 1) * group)
        cur = load_rows(0, cols)
        total = cur
        for j in range(1, w):
            total = total + load_rows(j, cols)
        cnt = jnp.minimum(pos + 1, w).astype(F32)
        d = total * (1.0 / cnt) - cur
        y = jnp.dot(d.astype(BF16), wpool_ref[g], preferred_element_type=F32)
        store(cols, y * scale_ref[:, cols])


def _pool_prompt_kernel(prev_ref, u_ref, wpool_ref, scale_ref, o_ref, ext_ref):
    tm = u_ref.shape[1]
    i = pl.program_id(1)
    p = POOL_PREV_ROWS
    ext_ref[0:p, :] = jnp.where(i == 0, 0.0, prev_ref[0])
    ext_ref[p:p + tm, :] = u_ref[0]
    pos = i * tm + lax.broadcasted_iota(jnp.int32, (tm, 1), 0)

    def store(cols, y):
        o_ref[0, :, cols] = y

    _pool_core(lambda j, cols: ext_ref[pl.ds(p - j, tm), cols], pos, wpool_ref, scale_ref, store)


def _pool_prompt(u, wpool_bf16, scale, tm):
    b, s, c = u.shape
    p = POOL_PREV_ROWS
    return pl.pallas_call(
        _pool_prompt_kernel,
        grid=(b, s // tm),
        in_specs=[pl.BlockSpec((1, p, c), lambda bi, i: (bi, jnp.maximum(i * (tm // p) - 1, 0), 0)),
                  pl.BlockSpec((1, tm, c), lambda bi, i: (bi, i, 0)),
                  _const_spec(wpool_bf16.shape), _const_spec((1, c))],
        out_specs=pl.BlockSpec((1, tm, c), lambda bi, i: (bi, i, 0)),
        out_shape=jax.ShapeDtypeStruct((b, s, c), F32),
        scratch_shapes=[pltpu.VMEM((p + tm, c), F32)],
        compiler_params=pltpu.CompilerParams(
            dimension_semantics=("arbitrary", "arbitrary"), vmem_limit_bytes=VMEM_LIMIT_BYTES),
        name="pool_prompt",
    )(u, u, wpool_bf16, scale)


def _pool_sample_kernel(past, ext_ref, wpool_ref, scale_ref, o_ref):
    n_new = o_ref.shape[1]
    pos = past + lax.broadcasted_iota(jnp.int32, (n_new, 1), 0)

    def store(cols, y):
        o_ref[0, :, cols] = y

    _pool_core(lambda j, cols: ext_ref[0, pl.ds(POOL_STATE - j, n_new), cols],
               pos, wpool_ref, scale_ref, store)


def _pool_sample(u_ext, wpool_bf16, scale, past):
    db, ext_rows, c = u_ext.shape
    n_new = ext_rows - POOL_STATE
    return pl.pallas_call(
        functools.partial(_pool_sample_kernel, past),
        grid=(db,),
        in_specs=[pl.BlockSpec((1, ext_rows, c), lambda bi: (bi, 0, 0)),
                  _const_spec(wpool_bf16.shape), _const_spec((1, c))],
        out_specs=pl.BlockSpec((1, n_new, c), lambda bi: (bi, 0, 0)),
        out_shape=jax.ShapeDtypeStruct((db, n_new, c), F32),
        compiler_params=pltpu.CompilerParams(
            dimension_semantics=("arbitrary",), vmem_limit_bytes=VMEM_LIMIT_BYTES),
        name="pool_sample",
    )(u_ext, wpool_bf16, scale)


def _finish_kernel(x_ref, oa_ref, op_ref, ga_ref, gb_ref, p_ref,
                   g_mix_ref, g_pre_ref, g_post_ref,
                   w_out_ref, w_up_ref, w_down_ref, w_proj_ref, w_gate_ref, y_ref):
    merged = jax.nn.sigmoid(ga_ref[...]) * oa_ref[...] + jax.nn.sigmoid(gb_ref[...]) * op_ref[...]
    mix = jnp.dot(merged.astype(BF16), w_out_ref[...], preferred_element_type=F32)
    h = x_ref[...] + _rms(mix, g_mix_ref[...])
    up = jnp.dot(_rms(h, g_pre_ref[...]).astype(BF16), w_up_ref[...], preferred_element_type=F32)
    hid = jnp.square(jnp.maximum(up, 0.0)).astype(BF16)
    down = jnp.dot(hid, w_down_ref[...], preferred_element_type=F32)
    h = h + _rms(down, g_post_ref[...])
    gate = jax.nn.sigmoid(jnp.dot(h.astype(BF16), w_gate_ref[...], preferred_element_type=F32))
    ple = jnp.dot(p_ref[...].astype(BF16), w_proj_ref[...], preferred_element_type=F32)
    y_ref[...] = h + gate * ple


def _finish(x, o_att, o_pool, ga, gb, p, gains, weights, tm):
    rows, d = x.shape
    row_spec = pl.BlockSpec((tm, d), lambda i: (i, 0))
    p_spec = pl.BlockSpec((tm, p.shape[1]), lambda i: (i, 0))
    return pl.pallas_call(
        _finish_kernel,
        grid=(rows // tm,),
        in_specs=[row_spec] * 5 + [p_spec] + [_const_spec((1, d))] * 3
                 + [_const_spec(w.shape) for w in weights],
        out_specs=row_spec,
        out_shape=jax.ShapeDtypeStruct((rows, d), F32),
        compiler_params=pltpu.CompilerParams(
            dimension_semantics=("arbitrary",), vmem_limit_bytes=VMEM_LIMIT_BYTES),
        name="finish",
    )(x, o_att, o_pool, ga, gb, p, *gains, *weights)


def _feature_major_pages(cache):
    n_pool, page, heads, dim = cache.shape
    return cache.transpose(0, 2, 3, 1).reshape(n_pool, heads * dim, page)


def _token_major(x_t, b, s):
    return x_t.reshape(b, N_HEADS, HEAD_DIM, s).transpose(0, 3, 1, 2)[None]


def kernel(x_prompt, x_sample, cache_k, cache_v, state_pool, page_table, p_prompt, p_sample,
           norm_pre_mix, norm_post_mix, norm_pre_mlp, norm_post_mlp, w_in, sb_bias, w_pool, pool_scale,
           w_out, w_up, w_down, w_ple_proj, w_ple_gate):
    depth = w_in.shape[0]
    assert depth == 1, "single-layer step"
    b, s, d = x_prompt.shape
    db, dt, _ = x_sample.shape
    past = page_table.shape[1] * PAGE_SIZE
    l = 0

    w_in_b = w_in[l].astype(BF16)
    w_rows = jnp.concatenate([w_in_b[:, d:2 * d], w_in_b[:, 3 * d:]], axis=1)
    w_t = w_in_b[:, :3 * d].T
    w_pool_b = w_pool[l].astype(BF16)
    weights = tuple(w[l].astype(BF16) for w in (w_out, w_up, w_down, w_ple_proj, w_ple_gate))
    gains = (norm_post_mix[l][None], norm_pre_mlp[l][None], norm_post_mlp[l][None])
    g_in = norm_pre_mix[l][None]
    scale = pool_scale[l][None]

    qt, kt, vt, u, ga, gb, kb, vtb = _project(x_prompt, g_in, w_rows, w_t, 256, True)
    rows_s = db * dt
    xs = x_sample.reshape(1, rows_s, d)
    qs, ks, vs, us, gas, gbs, _, _ = _project(xs, g_in, w_rows, w_t, rows_s, False)
    ks, vs, us = (t.reshape(db, dt, d) for t in (ks, vs, us))

    q_heads = (qs.reshape(db, dt, N_HEADS, 1, HEAD_DIM) * jnp.eye(N_HEADS, dtype=BF16)[None, None, :, :, None]
               ).reshape(db, dt * N_HEADS, d)
    bias_col = jnp.tile(sb_bias[l], dt)[:, None]

    def own_page(t):
        return jnp.pad(t.transpose(0, 2, 1), ((0, 0), (0, 0), (0, PAGE_SIZE - dt)))

    o_att, o_att_s = _attention(qt, kb, vtb, sb_bias[l], q_heads, bias_col, own_page(ks), own_page(vs),
                                _feature_major_pages(cache_k[l]), _feature_major_pages(cache_v[l]), page_table)

    o_pool = _pool_prompt(u, w_pool_b, scale, 256)
    rows_p = b * s
    y_prompt = _finish(x_prompt.reshape(rows_p, d), o_att.reshape(rows_p, d), o_pool.reshape(rows_p, d),
                       ga.reshape(rows_p, d), gb.reshape(rows_p, d), p_prompt[l].reshape(rows_p, -1),
                       gains, weights, 256).reshape(b, s, d)

    u_ext = jnp.concatenate([state_pool[l], us], axis=1)
    o_pool_s = _pool_sample(u_ext, w_pool_b, scale, past)
    y_sample = _finish(xs.reshape(rows_s, d), o_att_s.reshape(rows_s, d), o_pool_s.reshape(rows_s, d),
                       gas.reshape(rows_s, d), gbs.reshape(rows_s, d),
                       p_sample[l].reshape(rows_s, -1), gains, weights, rows_s).reshape(db, dt, d)

    heads = (N_HEADS, HEAD_DIM)
    return (y_prompt, y_sample, _token_major(kt, b, s), _token_major(vt, b, s),
            ks.reshape(1, db, dt, *heads), vs.reshape(1, db, dt, *heads),
            u[:, s - POOL_STATE:][None], u_ext[:, dt:][None])
```

```python
import functools

import jax
import jax.numpy as jnp
from jax import lax
from jax.experimental import pallas as pl
from jax.experimental.pallas import tpu as pltpu

N_HEADS = 16
HEAD_DIM = 64
POOL_WINDOWS = (2, 4, 8, 16)
POOL_STATE = max(POOL_WINDOWS) - 1
PAGE_SIZE = 128
RMS_EPS = 1e-6

F32 = jnp.float32
BF16 = jnp.bfloat16

VMEM_LIMIT_BYTES = 56 * 1024 * 1024
ATTN_BLOCK = 256
HEADS_PER_STEP = 16
HEADS_PER_DOT = 4
PAGES_PER_STEP = 8
PAGES_PER_CHUNK = 2
POOL_PREV_ROWS = 16

NT_DIMS = (((1,), (1,)), ((), ()))


def _rms(x, g):
    return x * lax.rsqrt(jnp.mean(x * x, axis=-1, keepdims=True) + RMS_EPS) * g


def _softplus(z):
    neg_abs = pltpu.bitcast(pltpu.bitcast(z, jnp.uint32) | jnp.uint32(0x80000000), F32)
    return jnp.maximum(z, 0.0) + jnp.log(1.0 + jnp.exp(neg_abs))


def _neg_suffix_matrix(n):
    rows = lax.broadcasted_iota(jnp.int32, (n, n), 0)
    cols = lax.broadcasted_iota(jnp.int32, (n, n), 1)
    return jnp.where(rows >= cols, -1.0, 0.0).astype(BF16)


def _const_spec(shape):
    return pl.BlockSpec(shape, lambda *_: (0,) * len(shape), pipeline_mode=pl.Buffered(1))


def _project_kernel(feature_major, x_ref, g_ref, w_ref, wt_ref,
                    q_ref, k_ref, v_ref, u_ref, ga_ref, gb_ref, kb_ref, vb_ref):
    d = x_ref.shape[2]
    xn = _rms(x_ref[0], g_ref[...]).astype(BF16)

    def col(i):
        return jnp.dot(xn, w_ref[:, i * d:(i + 1) * d], preferred_element_type=F32)

    def qkv(i):
        w_t = wt_ref[i * d:(i + 1) * d, :]
        if feature_major:
            return lax.dot_general(w_t, xn, NT_DIMS, preferred_element_type=F32)
        return lax.dot_general(xn, w_t, NT_DIMS, preferred_element_type=F32)

    q_ref[0] = (qkv(0) * (HEAD_DIM ** -0.5)).astype(BF16)
    k = qkv(1)
    k_ref[0] = k
    kb_ref[0] = (col(0) if feature_major else k).astype(BF16)
    v = qkv(2)
    v_ref[0] = v
    vb_ref[0] = v.astype(BF16)
    u_ref[0] = col(1)
    ga_ref[0] = col(2)
    gb_ref[0] = col(3)


def _project(x, g, w_bf16, wt_bf16, tm, feature_major):
    b, s, d = x.shape
    row_spec = pl.BlockSpec((1, tm, d), lambda bi, i: (bi, i, 0))
    if feature_major:
        fm_spec = pl.BlockSpec((1, d, tm), lambda bi, i: (bi, 0, i))
        fm_shape = (b, d, s)
    else:
        fm_spec, fm_shape = row_spec, (b, s, d)
    rows_f32 = jax.ShapeDtypeStruct((b, s, d), F32)
    return pl.pallas_call(
        functools.partial(_project_kernel, feature_major),
        grid=(b, s // tm),
        in_specs=[row_spec, _const_spec((1, d)), _const_spec(w_bf16.shape), _const_spec(wt_bf16.shape)],
        out_specs=[fm_spec, fm_spec, fm_spec, row_spec, row_spec, row_spec, row_spec, fm_spec],
        out_shape=[jax.ShapeDtypeStruct(fm_shape, BF16),
                   jax.ShapeDtypeStruct(fm_shape, F32), jax.ShapeDtypeStruct(fm_shape, F32),
                   rows_f32, rows_f32, rows_f32,
                   jax.ShapeDtypeStruct((b, s, d), BF16), jax.ShapeDtypeStruct(fm_shape, BF16)],
        compiler_params=pltpu.CompilerParams(
            dimension_semantics=("arbitrary", "arbitrary"), vmem_limit_bytes=VMEM_LIMIT_BYTES),
        name="project",
    )(x, g, w_bf16, wt_bf16)


def _attention_kernel(n_groups, n_live, pt_ref, qi_ref, kb_ref,
                      bias_ref, qt_ref, k_ref, vt_ref,
                      qs_ref, bias_col_ref, kn_ref, vn_ref, *refs):
    del pt_ref
    g = PAGES_PER_STEP
    kp_refs, vp_refs = refs[:g], refs[g:2 * g]
    o_ref, os_ref, q_scr, o_acc, c_acc, s_acc, s_c = refs[2 * g:]
    t = ATTN_BLOCK
    hg = pl.program_id(1)
    st = pl.program_id(2)
    qi = qi_ref[st]
    kb = kb_ref[st]
    it = (pl.program_id(0) * pl.num_programs(1) + hg) * pl.num_programs(2) + st
    group = it % n_groups
    live = it < n_live

    rows = lax.broadcasted_iota(jnp.int32, (t, t), 0)
    cols = lax.broadcasted_iota(jnp.int32, (t, t), 1)
    neg_suffix_t = jnp.where(cols >= rows, -1.0, 0.0).astype(BF16)
    causal = rows < cols
    heads = range(HEADS_PER_STEP)
    feats = [slice(hh * HEAD_DIM, (hh + 1) * HEAD_DIM) for hh in heads]
    biases = [bias_ref[hg * HEADS_PER_STEP + hh] for hh in heads]
    depth = HEADS_PER_DOT * HEAD_DIM
    group_feats = [slice(hh // HEADS_PER_DOT * depth, (hh // HEADS_PER_DOT + 1) * depth) for hh in heads]

    def head_queries(hh):
        feat_head = lax.broadcasted_iota(jnp.int32, (depth, t), 0) // HEAD_DIM
        return jnp.where(feat_head == hh % HEADS_PER_DOT, qt_ref[0, group_feats[hh], :].astype(F32), 0.0
                         ).astype(BF16)

    def sample_logits(kt_refs, own):
        kt = jnp.concatenate([r[0].astype(BF16) for r in kt_refs], axis=1)
        z = jnp.dot(qs_ref[0], kt, preferred_element_type=F32) + bias_col_ref[...]
        readable = None
        if own:
            token = lax.broadcasted_iota(jnp.int32, z.shape, 0) // N_HEADS
            readable = lax.broadcasted_iota(jnp.int32, z.shape, 1) < token
        return z, readable

    def sample_suffix(sp, n_pages):
        n = PAGE_SIZE * min(PAGES_PER_CHUNK, n_pages)
        neg_suffix = _neg_suffix_matrix(n)
        incls = [jnp.dot(sp[:, i:i + n].astype(BF16), neg_suffix, preferred_element_type=F32)
                 for i in range(0, PAGE_SIZE * n_pages, n)]
        c = s_c[...]
        for i in reversed(range(len(incls))):
            incls[i] = incls[i] + c
            c = incls[i][:, 0:1]
        s_c[...] = c
        return jnp.concatenate(incls, axis=1)

    def sample_values(w, vt_refs):
        vt = jnp.concatenate([r[0].astype(BF16) for r in vt_refs], axis=1)
        s_acc[...] += lax.dot_general(w.astype(BF16), vt, NT_DIMS, preferred_element_type=F32)

    @pl.when(jnp.logical_and(live, group == 0))
    def _():
        s_acc[...] = jnp.zeros_like(s_acc)
        s_c[...] = jnp.zeros_like(s_c)
        z, readable = sample_logits([kn_ref], True)
        incl = sample_suffix(jnp.where(readable, _softplus(z), 0.0), 1)
        sample_values(jnp.where(readable, jnp.exp(z + incl), 0.0), [vn_ref])

    def step(masked):
        if masked:
            for hh in heads:
                q_scr[hh] = head_queries(hh)
        zs = [jnp.dot(k_ref[0, :, group_feats[hh]], q_scr[hh], preferred_element_type=F32) + biases[hh]
              for hh in heads]
        z_s, _ = sample_logits(kp_refs, False)
        sps = [_softplus(z) for z in zs]
        if masked:
            sps = [jnp.where(causal, sp, 0.0) for sp in sps]
        sp_s = _softplus(z_s)
        incls = [jnp.dot(neg_suffix_t, sps[hh].astype(BF16), preferred_element_type=F32) for hh in heads]
        if not masked:
            incls = [incls[hh] + c_acc[hh:hh + 1, :] for hh in heads]
        incl_s = sample_suffix(sp_s, g)
        ws = [jnp.exp(zs[hh] + incls[hh]) for hh in heads]
        if masked:
            ws = [jnp.where(causal, w, 0.0) for w in ws]
        w_s = jnp.exp(z_s + incl_s)
        for hh in heads:
            o_t = jnp.dot(vt_ref[0, feats[hh], :], ws[hh].astype(BF16), preferred_element_type=F32)
            if masked:
                o_acc[feats[hh], :] = o_t
            else:
                o_acc[feats[hh], :] += o_t
            c_acc[hh:hh + 1, :] = incls[hh][0:1, :]
        sample_values(w_s, vp_refs)

    pl.when(kb == qi)(lambda: step(True))
    pl.when(kb != qi)(lambda: step(False))

    @pl.when(kb == 0)
    def _():
        o_ref[0] = o_acc[...].T

    @pl.when(jnp.logical_and(live, group == n_groups - 1))
    def _():
        width = s_acc.shape[1]
        shape = (N_HEADS, width)
        own_head = (lax.broadcasted_iota(jnp.int32, shape, 0)
                    == lax.broadcasted_iota(jnp.int32, shape, 1) // HEAD_DIM)
        for tok in range(os_ref.shape[1]):
            blk = s_acc[tok * N_HEADS:(tok + 1) * N_HEADS, :]
            os_ref[0, tok:tok + 1, :] = jnp.sum(jnp.where(own_head, blk, 0.0), axis=0, keepdims=True)


def _attention(qt, kb, vtb, sb_bias, q_heads, bias_col, kt_own, vt_own, cache_kt, cache_vt, page_table):
    b, s, d = kb.shape
    db, n_pages = page_table.shape
    _, hq, _ = q_heads.shape
    t = ATTN_BLOCK
    g = PAGES_PER_STEP
    width = HEADS_PER_STEP * HEAD_DIM
    n_hg = d // width
    n_groups = n_pages // g
    n_live = db * n_groups
    tiles = [(qi, kb_) for qi in range(s // t) for kb_ in range(qi, -1, -1)]
    n_st = len(tiles)
    assert b * n_hg * n_st >= n_live, "not enough prompt tiles to carry the page stream"
    qi_tab = jnp.asarray([qi for qi, _ in tiles], jnp.int32)
    kb_tab = jnp.asarray([kb_ for _, kb_ in tiles], jnp.int32)

    def stream(bi, hg, st):
        it = jnp.minimum((bi * n_hg + hg) * n_st + st, n_live - 1)
        return it // n_groups, it % n_groups

    def seq_spec(shape):
        return pl.BlockSpec(shape, lambda bi, hg, st, pt, qi_t, kb_t: (stream(bi, hg, st)[0], 0, 0))

    def page_spec(slot):
        def index(bi, hg, st, pt, qi_t, kb_t):
            seq, grp = stream(bi, hg, st)
            return pt[seq, (n_groups - 1 - grp) * g + slot], 0, 0
        return pl.BlockSpec((1, d, PAGE_SIZE), index)

    grid_spec = pltpu.PrefetchScalarGridSpec(
        num_scalar_prefetch=3,
        grid=(b, n_hg, n_st),
        in_specs=[pl.BlockSpec(memory_space=pltpu.SMEM),
                  pl.BlockSpec((1, width, t), lambda bi, hg, st, pt, qi_t, kb_t: (bi, hg, qi_t[st])),
                  pl.BlockSpec((1, t, width), lambda bi, hg, st, pt, qi_t, kb_t: (bi, kb_t[st], hg)),
                  pl.BlockSpec((1, width, t), lambda bi, hg, st, pt, qi_t, kb_t: (bi, hg, kb_t[st])),
                  seq_spec((1, hq, d)),
                  pl.BlockSpec((hq, 1), lambda *_: (0, 0)),
                  seq_spec((1, d, PAGE_SIZE)), seq_spec((1, d, PAGE_SIZE))]
                 + [page_spec(slot) for slot in range(g)] * 2,
        out_specs=[pl.BlockSpec((1, t, width), lambda bi, hg, st, pt, qi_t, kb_t: (bi, qi_t[st], hg)),
                   seq_spec((1, hq // N_HEADS, d))],
        scratch_shapes=[pltpu.VMEM((HEADS_PER_STEP, HEADS_PER_DOT * HEAD_DIM, t), BF16),
                        pltpu.VMEM((width, t), F32), pltpu.VMEM((HEADS_PER_STEP, t), F32),
                        pltpu.VMEM((hq, d), F32), pltpu.VMEM((hq, 1), F32)],
    )
    return pl.pallas_call(
        functools.partial(_attention_kernel, n_groups, n_live),
        grid_spec=grid_spec,
        out_shape=[jax.ShapeDtypeStruct((b, s, d), F32),
                   jax.ShapeDtypeStruct((db, hq // N_HEADS, d), F32)],
        compiler_params=pltpu.CompilerParams(
            dimension_semantics=("arbitrary", "arbitrary", "arbitrary"),
            vmem_limit_bytes=VMEM_LIMIT_BYTES),
        name="attention",
    )(page_table, qi_tab, kb_tab, sb_bias, qt, kb, vtb, q_heads, bias_col, kt_own, vt_own,
      *([cache_kt] * g), *([cache_vt] * g))


def _pool_core(load_rows, pos, wpool_ref, scale_ref, store):
    group = wpool_ref.shape[1]
    for g, w in enumerate(POOL_WINDOWS):
        cols = slice(g * group, (g + 1) * group)
        cur = load_rows(0, cols)
        total = cur
        for j in range(1, w):
            total = total + load_rows(j, cols)
        cnt = jnp.minimum(pos + 1, w).astype(F32)
        d = total * (1.0 / cnt) - cur
        y = jnp.dot(d.astype(BF16), wpool_ref[g], preferred_element_type=F32)
        store(cols, y * scale_ref[:, cols])


def _pool_sample_kernel(past, ext_ref, wpool_ref, scale_ref, o_ref):
    n_new = o_ref.shape[1]
    pos = past + lax.broadcasted_iota(jnp.int32, (n_new, 1), 0)

    def store(cols, y):
        o_ref[0, :, cols] = y

    _pool_core(lambda j, cols: ext_ref[0, pl.ds(POOL_STATE - j, n_new), cols],
               pos, wpool_ref, scale_ref, store)


def _pool_sample(u_ext, wpool_bf16, scale, past):
    db, ext_rows, c = u_ext.shape
    n_new = ext_rows - POOL_STATE
    return pl.pallas_call(
        functools.partial(_pool_sample_kernel, past),
        grid=(db,),
        in_specs=[pl.BlockSpec((1, ext_rows, c), lambda bi: (bi, 0, 0)),
                  _const_spec(wpool_bf16.shape), _const_spec((1, c))],
        out_specs=pl.BlockSpec((1, n_new, c), lambda bi: (bi, 0, 0)),
        out_shape=jax.ShapeDtypeStruct((db, n_new, c), F32),
        compiler_params=pltpu.CompilerParams(
            dimension_semantics=("arbitrary",), vmem_limit_bytes=VMEM_LIMIT_BYTES),
        name="pool_sample",
    )(u_ext, wpool_bf16, scale)


def _finish_kernel(tiles_per_seq, x_ref, oa_ref, ga_ref, gb_ref, p_ref, g_mix_ref, g_pre_ref, g_post_ref,
                   w_out_ref, w_up_ref, w_down_ref, w_proj_ref, w_gate_ref, *refs):
    if tiles_per_seq is None:
        op_ref, y_ref = refs
        o_pool = op_ref[...]
    else:
        prev_ref, u_ref, wpool_ref, scale_ref, y_ref, ext_ref, pool_ref = refs
        tm = u_ref.shape[0]
        p = POOL_PREV_ROWS
        tile = pl.program_id(0) % tiles_per_seq
        ext_ref[0:p, :] = jnp.where(tile == 0, 0.0, prev_ref[...])
        ext_ref[p:p + tm, :] = u_ref[...]
        pos = tile * tm + lax.broadcasted_iota(jnp.int32, (tm, 1), 0)

        def store(cols, y):
            pool_ref[:, cols] = y

        _pool_core(lambda j, cols: ext_ref[pl.ds(p - j, tm), cols], pos, wpool_ref, scale_ref, store)
        o_pool = pool_ref[...]
    merged = jax.nn.sigmoid(ga_ref[...]) * oa_ref[...] + jax.nn.sigmoid(gb_ref[...]) * o_pool
    mix = jnp.dot(merged.astype(BF16), w_out_ref[...], preferred_element_type=F32)
    h = x_ref[...] + _rms(mix, g_mix_ref[...])
    up = jnp.dot(_rms(h, g_pre_ref[...]).astype(BF16), w_up_ref[...], preferred_element_type=F32)
    hid = jnp.square(jnp.maximum(up, 0.0)).astype(BF16)
    down = jnp.dot(hid, w_down_ref[...], preferred_element_type=F32)
    h = h + _rms(down, g_post_ref[...])
    gate = jax.nn.sigmoid(jnp.dot(h.astype(BF16), w_gate_ref[...], preferred_element_type=F32))
    ple = jnp.dot(p_ref[...].astype(BF16), w_proj_ref[...], preferred_element_type=F32)
    y_ref[...] = h + gate * ple


def _finish(x, o_att, ga, gb, p, gains, weights, tm, o_pool=None, pool_inputs=None):
    rows, d = x.shape
    row_spec = pl.BlockSpec((tm, d), lambda i: (i, 0))
    p_spec = pl.BlockSpec((tm, p.shape[1]), lambda i: (i, 0))
    in_specs = ([row_spec] * 4 + [p_spec] + [_const_spec((1, d))] * 3
                + [_const_spec(w.shape) for w in weights])
    if pool_inputs is None:
        tiles_per_seq, extra, extra_specs, scratch = None, (o_pool,), [row_spec], []
    else:
        u, wpool_bf16, scale, seq_rows = pool_inputs
        prev = POOL_PREV_ROWS
        tiles_per_seq = seq_rows // tm
        extra = (u, u, wpool_bf16, scale)
        extra_specs = [pl.BlockSpec((prev, d), lambda i: (jnp.maximum(i * (tm // prev) - 1, 0), 0)),
                       row_spec, _const_spec(wpool_bf16.shape), _const_spec((1, d))]
        scratch = [pltpu.VMEM((prev + tm, d), F32), pltpu.VMEM((tm, d), F32)]
    return pl.pallas_call(
        functools.partial(_finish_kernel, tiles_per_seq),
        grid=(rows // tm,),
        in_specs=in_specs + extra_specs,
        out_specs=row_spec,
        out_shape=jax.ShapeDtypeStruct((rows, d), F32),
        scratch_shapes=scratch,
        compiler_params=pltpu.CompilerParams(
            dimension_semantics=("arbitrary",), vmem_limit_bytes=VMEM_LIMIT_BYTES),
        name="finish",
    )(x, o_att, ga, gb, p, *gains, *weights, *extra)


def _feature_major_pages(cache):
    n_pool, page, heads, dim = cache.shape
    return cache.transpose(0, 2, 3, 1).reshape(n_pool, heads * dim, page)


def _token_major(x_t, b, s):
    return x_t.reshape(b, N_HEADS, HEAD_DIM, s).transpose(0, 3, 1, 2)[None]


def kernel(x_prompt, x_sample, cache_k, cache_v, state_pool, page_table, p_prompt, p_sample,
           norm_pre_mix, norm_post_mix, norm_pre_mlp, norm_post_mlp, w_in, sb_bias, w_pool, pool_scale,
           w_out, w_up, w_down, w_ple_proj, w_ple_gate):
    depth = w_in.shape[0]
    assert depth == 1, "single-layer step"
    b, s, d = x_prompt.shape
    db, dt, _ = x_sample.shape
    past = page_table.shape[1] * PAGE_SIZE
    l = 0

    w_in_b = w_in[l].astype(BF16)
    w_rows = jnp.concatenate([w_in_b[:, d:2 * d], w_in_b[:, 3 * d:]], axis=1)
    w_t = w_in_b[:, :3 * d].T
    w_pool_b = w_pool[l].astype(BF16)
    weights = tuple(w[l].astype(BF16) for w in (w_out, w_up, w_down, w_ple_proj, w_ple_gate))
    gains = (norm_post_mix[l][None], norm_pre_mlp[l][None], norm_post_mlp[l][None])
    g_in = norm_pre_mix[l][None]
    scale = pool_scale[l][None]

    qt, kt, vt, u, ga, gb, kb, vtb = _project(x_prompt, g_in, w_rows, w_t, 256, True)
    rows_s = db * dt
    xs = x_sample.reshape(1, rows_s, d)
    qs, ks, vs, us, gas, gbs, _, _ = _project(xs, g_in, w_rows, w_t, rows_s, False)
    ks, vs, us = (t.reshape(db, dt, d) for t in (ks, vs, us))

    q_heads = (qs.reshape(db, dt, N_HEADS, 1, HEAD_DIM) * jnp.eye(N_HEADS, dtype=BF16)[None, None, :, :, None]
               ).reshape(db, dt * N_HEADS, d)
    bias_col = jnp.tile(sb_bias[l], dt)[:, None]

    def own_page(t):
        return jnp.pad(t.transpose(0, 2, 1), ((0, 0), (0, 0), (0, PAGE_SIZE - dt)))

    o_att, o_att_s = _attention(qt, kb, vtb, sb_bias[l], q_heads, bias_col, own_page(ks), own_page(vs),
                                _feature_major_pages(cache_k[l]), _feature_major_pages(cache_v[l]), page_table)

    rows_p = b * s
    y_prompt = _finish(x_prompt.reshape(rows_p, d), o_att.reshape(rows_p, d),
                       ga.reshape(rows_p, d), gb.reshape(rows_p, d), p_prompt[l].reshape(rows_p, -1),
                       gains, weights, 256,
                       pool_inputs=(u.reshape(rows_p, d), w_pool_b, scale, s)).reshape(b, s, d)

    u_ext = jnp.concatenate([state_pool[l], us], axis=1)
    o_pool_s = _pool_sample(u_ext, w_pool_b, scale, past)
    y_sample = _finish(xs.reshape(rows_s, d), o_att_s.reshape(rows_s, d),
                       gas.reshape(rows_s, d), gbs.reshape(rows_s, d),
                       p_sample[l].reshape(rows_s, -1), gains, weights, rows_s,
                       o_pool=o_pool_s.reshape(rows_s, d)).reshape(db, dt, d)

    heads = (N_HEADS, HEAD_DIM)
    return (y_prompt, y_sample, _token_major(kt, b, s), _token_major(vt, b, s),
            ks.reshape(1, db, dt, *heads), vs.reshape(1, db, dt, *heads),
            u[:, s - POOL_STATE:][None], u_ext[:, dt:][None])
```

```python
import functools

import jax
import jax.numpy as jnp
from jax import lax
from jax.experimental import pallas as pl
from jax.experimental.pallas import tpu as pltpu

N_HEADS = 16
HEAD_DIM = 64
POOL_WINDOWS = (2, 4, 8, 16)
POOL_STATE = max(POOL_WINDOWS) - 1
PAGE_SIZE = 128
RMS_EPS = 1e-6

F32 = jnp.float32
BF16 = jnp.bfloat16

VMEM_LIMIT_BYTES = 56 * 1024 * 1024
ATTN_BLOCK = 256
HEADS_PER_STEP = 16
HEADS_PER_DOT = 4
HEADS_PER_WAVE = 4
PAGES_PER_STEP = 8
PAGES_PER_CHUNK = 2
POOL_PREV_ROWS = 16

NT_DIMS = (((1,), (1,)), ((), ()))


def _rms(x, g):
    return x * lax.rsqrt(jnp.mean(x * x, axis=-1, keepdims=True) + RMS_EPS) * g


def _softplus(z):
    neg_abs = pltpu.bitcast(pltpu.bitcast(z, jnp.uint32) | jnp.uint32(0x80000000), F32)
    return jnp.maximum(z, 0.0) + jnp.log(1.0 + jnp.exp(neg_abs))


def _neg_suffix_matrix(n):
    rows = lax.broadcasted_iota(jnp.int32, (n, n), 0)
    cols = lax.broadcasted_iota(jnp.int32, (n, n), 1)
    return jnp.where(rows >= cols, -1.0, 0.0).astype(BF16)


def _const_spec(shape):
    return pl.BlockSpec(shape, lambda *_: (0,) * len(shape), pipeline_mode=pl.Buffered(1))


def _project_kernel(feature_major, x_ref, g_ref, w_ref, wt_ref,
                    q_ref, k_ref, v_ref, u_ref, ga_ref, gb_ref, kb_ref, vb_ref):
    d = x_ref.shape[2]
    xn = _rms(x_ref[0], g_ref[...]).astype(BF16)

    def col(i):
        return jnp.dot(xn, w_ref[:, i * d:(i + 1) * d], preferred_element_type=F32)

    def qkv(i):
        w_t = wt_ref[i * d:(i + 1) * d, :]
        if feature_major:
            return lax.dot_general(w_t, xn, NT_DIMS, preferred_element_type=F32)
        return lax.dot_general(xn, w_t, NT_DIMS, preferred_element_type=F32)

    q_ref[0] = (qkv(0) * (HEAD_DIM ** -0.5)).astype(BF16)
    k = qkv(1)
    k_ref[0] = k
    kb_ref[0] = (col(0) if feature_major else k).astype(BF16)
    v = qkv(2)
    v_ref[0] = v
    vb_ref[0] = v.astype(BF16)
    u_ref[0] = col(1)
    ga_ref[0] = col(2)
    gb_ref[0] = col(3)


def _project(x, g, w_bf16, wt_bf16, tm, feature_major):
    b, s, d = x.shape
    row_spec = pl.BlockSpec((1, tm, d), lambda bi, i: (bi, i, 0))
    if feature_major:
        fm_spec = pl.BlockSpec((1, d, tm), lambda bi, i: (bi, 0, i))
        fm_shape = (b, d, s)
    else:
        fm_spec, fm_shape = row_spec, (b, s, d)
    rows_f32 = jax.ShapeDtypeStruct((b, s, d), F32)
    return pl.pallas_call(
        functools.partial(_project_kernel, feature_major),
        grid=(b, s // tm),
        in_specs=[row_spec, _const_spec((1, d)), _const_spec(w_bf16.shape), _const_spec(wt_bf16.shape)],
        out_specs=[fm_spec, fm_spec, fm_spec, row_spec, row_spec, row_spec, row_spec, fm_spec],
        out_shape=[jax.ShapeDtypeStruct(fm_shape, BF16),
                   jax.ShapeDtypeStruct(fm_shape, F32), jax.ShapeDtypeStruct(fm_shape, F32),
                   rows_f32, rows_f32, rows_f32,
                   jax.ShapeDtypeStruct((b, s, d), BF16), jax.ShapeDtypeStruct(fm_shape, BF16)],
        compiler_params=pltpu.CompilerParams(
            dimension_semantics=("arbitrary", "arbitrary"), vmem_limit_bytes=VMEM_LIMIT_BYTES),
        name="project",
    )(x, g, w_bf16, wt_bf16)


def _attention_kernel(n_groups, n_live, pt_ref, qi_ref, kb_ref,
                      bias_ref, qt_ref, k_ref, vt_ref,
                      qs_ref, bias_col_ref, kn_ref, vn_ref, *refs):
    del pt_ref
    g = PAGES_PER_STEP
    kp_refs, vp_refs = refs[:g], refs[g:2 * g]
    o_ref, os_ref, q_scr, o_acc, c_acc, s_acc, s_c = refs[2 * g:]
    t = ATTN_BLOCK
    hg = pl.program_id(1)
    st = pl.program_id(2)
    qi = qi_ref[st]
    kb = kb_ref[st]
    it = (pl.program_id(0) * pl.num_programs(1) + hg) * pl.num_programs(2) + st
    group = it % n_groups
    live = it < n_live

    rows = lax.broadcasted_iota(jnp.int32, (t, t), 0)
    cols = lax.broadcasted_iota(jnp.int32, (t, t), 1)
    neg_suffix_t = jnp.where(cols >= rows, -1.0, 0.0).astype(BF16)
    causal = rows < cols
    heads = range(HEADS_PER_STEP)
    feats = [slice(hh * HEAD_DIM, (hh + 1) * HEAD_DIM) for hh in heads]
    biases = [bias_ref[hg * HEADS_PER_STEP + hh] for hh in heads]
    depth = HEADS_PER_DOT * HEAD_DIM
    group_feats = [slice(hh // HEADS_PER_DOT * depth, (hh // HEADS_PER_DOT + 1) * depth) for hh in heads]

    def head_queries(hh):
        feat_head = lax.broadcasted_iota(jnp.int32, (depth, t), 0) // HEAD_DIM
        return jnp.where(feat_head == hh % HEADS_PER_DOT, qt_ref[0, group_feats[hh], :].astype(F32), 0.0
                         ).astype(BF16)

    def sample_logits(kt_refs, own):
        kt = jnp.concatenate([r[0].astype(BF16) for r in kt_refs], axis=1)
        z = jnp.dot(qs_ref[0], kt, preferred_element_type=F32) + bias_col_ref[...]
        readable = None
        if own:
            token = lax.broadcasted_iota(jnp.int32, z.shape, 0) // N_HEADS
            readable = lax.broadcasted_iota(jnp.int32, z.shape, 1) < token
        return z, readable

    def sample_suffix(sp, n_pages):
        n = PAGE_SIZE * min(PAGES_PER_CHUNK, n_pages)
        neg_suffix = _neg_suffix_matrix(n)
        incls = [jnp.dot(sp[:, i:i + n].astype(BF16), neg_suffix, preferred_element_type=F32)
                 for i in range(0, PAGE_SIZE * n_pages, n)]
        c = s_c[...]
        for i in reversed(range(len(incls))):
            incls[i] = incls[i] + c
            c = incls[i][:, 0:1]
        s_c[...] = c
        return jnp.concatenate(incls, axis=1)

    def sample_values(w, vt_refs):
        vt = jnp.concatenate([r[0].astype(BF16) for r in vt_refs], axis=1)
        s_acc[...] += lax.dot_general(w.astype(BF16), vt, NT_DIMS, preferred_element_type=F32)

    @pl.when(jnp.logical_and(live, group == 0))
    def _():
        s_acc[...] = jnp.zeros_like(s_acc)
        s_c[...] = jnp.zeros_like(s_c)
        z, readable = sample_logits([kn_ref], True)
        incl = sample_suffix(jnp.where(readable, _softplus(z), 0.0), 1)
        sample_values(jnp.where(readable, jnp.exp(z + incl), 0.0), [vn_ref])

    def step(masked):
        if masked:
            for hh in heads:
                q_scr[hh] = head_queries(hh)
        val = {}

        def logits(hh):
            val["z", hh] = (jnp.dot(k_ref[0, :, group_feats[hh]], q_scr[hh], preferred_element_type=F32)
                            + biases[hh])

        def log_keep(hh):
            sp = _softplus(val["z", hh])
            val["sp", hh] = jnp.where(causal, sp, 0.0) if masked else sp

        def suffix(hh):
            incl = jnp.dot(neg_suffix_t, val.pop(("sp", hh)).astype(BF16), preferred_element_type=F32)
            val["incl", hh] = incl if masked else incl + c_acc[hh:hh + 1, :]

        def weights(hh):
            w = jnp.exp(val.pop(("z", hh)) + val["incl", hh])
            val["w", hh] = jnp.where(causal, w, 0.0) if masked else w

        def values(hh):
            o_t = jnp.dot(vt_ref[0, feats[hh], :], val.pop(("w", hh)).astype(BF16),
                          preferred_element_type=F32)
            if masked:
                o_acc[feats[hh], :] = o_t
            else:
                o_acc[feats[hh], :] += o_t
            c_acc[hh:hh + 1, :] = val.pop(("incl", hh))[0:1, :]

        def over(stage, hs):
            return lambda: [stage(hh) for hh in hs]

        sample_chain = [
            lambda: val.update(z_s=sample_logits(kp_refs, False)[0]),
            lambda: val.update(sp_s=_softplus(val["z_s"])),
            lambda: val.update(incl_s=sample_suffix(val.pop("sp_s"), g)),
            lambda: val.update(w_s=jnp.exp(val.pop("z_s") + val.pop("incl_s"))),
            lambda: sample_values(val.pop("w_s"), vp_refs),
        ]
        chains = [sample_chain] + [
            [over(stage, range(first, first + HEADS_PER_WAVE))
             for stage in (logits, log_keep, suffix, weights, values)]
            for first in range(0, HEADS_PER_STEP, HEADS_PER_WAVE)]
        for wave in range(len(chains) + len(sample_chain) - 1):
            for lag, chain in enumerate(chains):
                if 0 <= wave - lag < len(chain):
                    chain[wave - lag]()

    pl.when(kb == qi)(lambda: step(True))
    pl.when(kb != qi)(lambda: step(False))

    @pl.when(kb == 0)
    def _():
        o_ref[0] = o_acc[...].T

    @pl.when(jnp.logical_and(live, group == n_groups - 1))
    def _():
        width = s_acc.shape[1]
        shape = (N_HEADS, width)
        own_head = (lax.broadcasted_iota(jnp.int32, shape, 0)
                    == lax.broadcasted_iota(jnp.int32, shape, 1) // HEAD_DIM)
        for tok in range(os_ref.shape[1]):
            blk = s_acc[tok * N_HEADS:(tok + 1) * N_HEADS, :]
            os_ref[0, tok:tok + 1, :] = jnp.sum(jnp.where(own_head, blk, 0.0), axis=0, keepdims=True)


def _attention(qt, kb, vtb, sb_bias, q_heads, bias_col, kt_own, vt_own, cache_kt, cache_vt, page_table):
    b, s, d = kb.shape
    db, n_pages = page_table.shape
    _, hq, _ = q_heads.shape
    t = ATTN_BLOCK
    g = PAGES_PER_STEP
    width = HEADS_PER_STEP * HEAD_DIM
    n_hg = d // width
    n_groups = n_pages // g
    n_live = db * n_groups
    tiles = [(qi, kb_) for qi in range(s // t) for kb_ in range(qi, -1, -1)]
    n_st = len(tiles)
    assert b * n_hg * n_st >= n_live, "not enough prompt tiles to carry the page stream"
    qi_tab = jnp.asarray([qi for qi, _ in tiles], jnp.int32)
    kb_tab = jnp.asarray([kb_ for _, kb_ in tiles], jnp.int32)

    def stream(bi, hg, st):
        it = jnp.minimum((bi * n_hg + hg) * n_st + st, n_live - 1)
        return it // n_groups, it % n_groups

    def seq_spec(shape):
        return pl.BlockSpec(shape, lambda bi, hg, st, pt, qi_t, kb_t: (stream(bi, hg, st)[0], 0, 0))

    def page_spec(slot):
        def index(bi, hg, st, pt, qi_t, kb_t):
            seq, grp = stream(bi, hg, st)
            return pt[seq, (n_groups - 1 - grp) * g + slot], 0, 0
        return pl.BlockSpec((1, d, PAGE_SIZE), index)

    grid_spec = pltpu.PrefetchScalarGridSpec(
        num_scalar_prefetch=3,
        grid=(b, n_hg, n_st),
        in_specs=[pl.BlockSpec(memory_space=pltpu.SMEM),
                  pl.BlockSpec((1, width, t), lambda bi, hg, st, pt, qi_t, kb_t: (bi, hg, qi_t[st])),
                  pl.BlockSpec((1, t, width), lambda bi, hg, st, pt, qi_t, kb_t: (bi, kb_t[st], hg)),
                  pl.BlockSpec((1, width, t), lambda bi, hg, st, pt, qi_t, kb_t: (bi, hg, kb_t[st])),
                  seq_spec((1, hq, d)),
                  pl.BlockSpec((hq, 1), lambda *_: (0, 0)),
                  seq_spec((1, d, PAGE_SIZE)), seq_spec((1, d, PAGE_SIZE))]
                 + [page_spec(slot) for slot in range(g)] * 2,
        out_specs=[pl.BlockSpec((1, t, width), lambda bi, hg, st, pt, qi_t, kb_t: (bi, qi_t[st], hg)),
                   seq_spec((1, hq // N_HEADS, d))],
        scratch_shapes=[pltpu.VMEM((HEADS_PER_STEP, HEADS_PER_DOT * HEAD_DIM, t), BF16),
                        pltpu.VMEM((width, t), F32), pltpu.VMEM((HEADS_PER_STEP, t), F32),
                        pltpu.VMEM((hq, d), F32), pltpu.VMEM((hq, 1), F32)],
    )
    return pl.pallas_call(
        functools.partial(_attention_kernel, n_groups, n_live),
        grid_spec=grid_spec,
        out_shape=[jax.ShapeDtypeStruct((b, s, d), F32),
                   jax.ShapeDtypeStruct((db, hq // N_HEADS, d), F32)],
        compiler_params=pltpu.CompilerParams(
            dimension_semantics=("arbitrary", "arbitrary", "arbitrary"),
            vmem_limit_bytes=VMEM_LIMIT_BYTES),
        name="attention",
    )(page_table, qi_tab, kb_tab, sb_bias, qt, kb, vtb, q_heads, bias_col, kt_own, vt_own,
      *([cache_kt] * g), *([cache_vt] * g))


def _pool_core(load_rows, pos, wpool_ref, scale_ref, store):
    group = wpool_ref.shape[1]
    for g, w in enumerate(POOL_WINDOWS):
        cols = slice(g * group, (g + 1) * group)
        cur = load_rows(0, cols)
        total = cur
        for j in range(1, w):
            total = total + load_rows(j, cols)
        cnt = jnp.minimum(pos + 1, w).astype(F32)
        d = total * (1.0 / cnt) - cur
        y = jnp.dot(d.astype(BF16), wpool_ref[g], preferred_element_type=F32)
        store(cols, y * scale_ref[:, cols])


def _pool_sample_kernel(past, ext_ref, wpool_ref, scale_ref, o_ref):
    n_new = o_ref.shape[1]
    pos = past + lax.broadcasted_iota(jnp.int32, (n_new, 1), 0)

    def store(cols, y):
        o_ref[0, :, cols] = y

    _pool_core(lambda j, cols: ext_ref[0, pl.ds(POOL_STATE - j, n_new), cols],
               pos, wpool_ref, scale_ref, store)


def _pool_sample(u_ext, wpool_bf16, scale, past):
    db, ext_rows, c = u_ext.shape
    n_new = ext_rows - POOL_STATE
    return pl.pallas_call(
        functools.partial(_pool_sample_kernel, past),
        grid=(db,),
        in_specs=[pl.BlockSpec((1, ext_rows, c), lambda bi: (bi, 0, 0)),
                  _const_spec(wpool_bf16.shape), _const_spec((1, c))],
        out_specs=pl.BlockSpec((1, n_new, c), lambda bi: (bi, 0, 0)),
        out_shape=jax.ShapeDtypeStruct((db, n_new, c), F32),
        compiler_params=pltpu.CompilerParams(
            dimension_semantics=("arbitrary",), vmem_limit_bytes=VMEM_LIMIT_BYTES),
        name="pool_sample",
    )(u_ext, wpool_bf16, scale)


def _finish_kernel(tiles_per_seq, x_ref, oa_ref, ga_ref, gb_ref, p_ref, g_mix_ref, g_pre_ref, g_post_ref,
                   w_out_ref, w_up_ref, w_down_ref, w_proj_ref, w_gate_ref, *refs):
    if tiles_per_seq is None:
        op_ref, y_ref = refs
        o_pool = op_ref[...]
    else:
        prev_ref, u_ref, wpool_ref, scale_ref, y_ref, ext_ref, pool_ref = refs
        tm = u_ref.shape[0]
        p = POOL_PREV_ROWS
        tile = pl.program_id(0) % tiles_per_seq
        ext_ref[0:p, :] = jnp.where(tile == 0, 0.0, prev_ref[...])
        ext_ref[p:p + tm, :] = u_ref[...]
        pos = tile * tm + lax.broadcasted_iota(jnp.int32, (tm, 1), 0)

        def store(cols, y):
            pool_ref[:, cols] = y

        _pool_core(lambda j, cols: ext_ref[pl.ds(p - j, tm), cols], pos, wpool_ref, scale_ref, store)
        o_pool = pool_ref[...]
    merged = jax.nn.sigmoid(ga_ref[...]) * oa_ref[...] + jax.nn.sigmoid(gb_ref[...]) * o_pool
    mix = jnp.dot(merged.astype(BF16), w_out_ref[...], preferred_element_type=F32)
    h = x_ref[...] + _rms(mix, g_mix_ref[...])
    up = jnp.dot(_rms(h, g_pre_ref[...]).astype(BF16), w_up_ref[...], preferred_element_type=F32)
    hid = jnp.square(jnp.maximum(up, 0.0)).astype(BF16)
    down = jnp.dot(hid, w_down_ref[...], preferred_element_type=F32)
    h = h + _rms(down, g_post_ref[...])
    gate = jax.nn.sigmoid(jnp.dot(h.astype(BF16), w_gate_ref[...], preferred_element_type=F32))
    ple = jnp.dot(p_ref[...].astype(BF16), w_proj_ref[...], preferred_element_type=F32)
    y_ref[...] = h + gate * ple


def _finish(x, o_att, ga, gb, p, gains, weights, tm, o_pool=None, pool_inputs=None):
    rows, d = x.shape
    row_spec = pl.BlockSpec((tm, d), lambda i: (i, 0))
    p_spec = pl.BlockSpec((tm, p.shape[1]), lambda i: (i, 0))
    in_specs = ([row_spec] * 4 + [p_spec] + [_const_spec((1, d))] * 3
                + [_const_spec(w.shape) for w in weights])
    if pool_inputs is None:
        tiles_per_seq, extra, extra_specs, scratch = None, (o_pool,), [row_spec], []
    else:
        u, wpool_bf16, scale, seq_rows = pool_inputs
        prev = POOL_PREV_ROWS
        tiles_per_seq = seq_rows // tm
        extra = (u, u, wpool_bf16, scale)
        extra_specs = [pl.BlockSpec((prev, d), lambda i: (jnp.maximum(i * (tm // prev) - 1, 0), 0)),
                       row_spec, _const_spec(wpool_bf16.shape), _const_spec((1, d))]
        scratch = [pltpu.VMEM((prev + tm, d), F32), pltpu.VMEM((tm, d), F32)]
    return pl.pallas_call(
        functools.partial(_finish_kernel, tiles_per_seq),
        grid=(rows // tm,),
        in_specs=in_specs + extra_specs,
        out_specs=row_spec,
        out_shape=jax.ShapeDtypeStruct((rows, d), F32),
        scratch_shapes=scratch,
        compiler_params=pltpu.CompilerParams(
            dimension_semantics=("arbitrary",), vmem_limit_bytes=VMEM_LIMIT_BYTES),
        name="finish",
    )(x, o_att, ga, gb, p, *gains, *weights, *extra)


def _feature_major_pages(cache):
    n_pool, page, heads, dim = cache.shape
    return cache.transpose(0, 2, 3, 1).reshape(n_pool, heads * dim, page)


def _token_major(x_t, b, s):
    return x_t.reshape(b, N_HEADS, HEAD_DIM, s).transpose(0, 3, 1, 2)[None]


def kernel(x_prompt, x_sample, cache_k, cache_v, state_pool, page_table, p_prompt, p_sample,
           norm_pre_mix, norm_post_mix, norm_pre_mlp, norm_post_mlp, w_in, sb_bias, w_pool, pool_scale,
           w_out, w_up, w_down, w_ple_proj, w_ple_gate):
    depth = w_in.shape[0]
    assert depth == 1, "single-layer step"
    b, s, d = x_prompt.shape
    db, dt, _ = x_sample.shape
    past = page_table.shape[1] * PAGE_SIZE
    l = 0

    w_in_b = w_in[l].astype(BF16)
    w_rows = jnp.concatenate([w_in_b[:, d:2 * d], w_in_b[:, 3 * d:]], axis=1)
    w_t = w_in_b[:, :3 * d].T
    w_pool_b = w_pool[l].astype(BF16)
    weights = tuple(w[l].astype(BF16) for w in (w_out, w_up, w_down, w_ple_proj, w_ple_gate))
    gains = (norm_post_mix[l][None], norm_pre_mlp[l][None], norm_post_mlp[l][None])
    g_in = norm_pre_mix[l][None]
    scale = pool_scale[l][None]

    qt, kt, vt, u, ga, gb, kb, vtb = _project(x_prompt, g_in, w_rows, w_t, 256, True)
    rows_s = db * dt
    xs = x_sample.reshape(1, rows_s, d)
    qs, ks, vs, us, gas, gbs, _, _ = _project(xs, g_in, w_rows, w_t, rows_s, False)
    ks, vs, us = (t.reshape(db, dt, d) for t in (ks, vs, us))

    q_heads = (qs.reshape(db, dt, N_HEADS, 1, HEAD_DIM) * jnp.eye(N_HEADS, dtype=BF16)[None, None, :, :, None]
               ).reshape(db, dt * N_HEADS, d)
    bias_col = jnp.tile(sb_bias[l], dt)[:, None]

    def own_page(t):
        return jnp.pad(t.transpose(0, 2, 1), ((0, 0), (0, 0), (0, PAGE_SIZE - dt)))

    o_att, o_att_s = _attention(qt, kb, vtb, sb_bias[l], q_heads, bias_col, own_page(ks), own_page(vs),
                                _feature_major_pages(cache_k[l]), _feature_major_pages(cache_v[l]), page_table)

    rows_p = b * s
    y_prompt = _finish(x_prompt.reshape(rows_p, d), o_att.reshape(rows_p, d),
                       ga.reshape(rows_p, d), gb.reshape(rows_p, d), p_prompt[l].reshape(rows_p, -1),
                       gains, weights, 256,
                       pool_inputs=(u.reshape(rows_p, d), w_pool_b, scale, s)).reshape(b, s, d)

    u_ext = jnp.concatenate([state_pool[l], us], axis=1)
    o_pool_s = _pool_sample(u_ext, w_pool_b, scale, past)
    y_sample = _finish(xs.reshape(rows_s, d), o_att_s.reshape(rows_s, d),
                       gas.reshape(rows_s, d), gbs.reshape(rows_s, d),
                       p_sample[l].reshape(rows_s, -1), gains, weights, rows_s,
                       o_pool=o_pool_s.reshape(rows_s, d)).reshape(db, dt, d)

    heads = (N_HEADS, HEAD_DIM)
    return (y_prompt, y_sample, _token_major(kt, b, s), _token_major(vt, b, s),
            ks.reshape(1, db, dt, *heads), vs.reshape(1, db, dt, *heads),
            u[:, s - POOL_STATE:][None], u_ext[:, dt:][None])
```
